```python
import jax, jax.numpy as jnp
from jax import lax
import numpy as np

D_MODEL = 2048
BATCH = 4
SEQ = 2048
DEPTH = 2
DEC_BATCH = 128
DEC_SEQ = 8
PAST_LEN = 2048
PAGE_SIZE = 128

GLA_HEADS = 4
GLA_DK = 128
GLA_DV = 256
GLA_LR = 16
GLA_TAU = 16.0
GLA_CHUNK = 64
GLA_WIDTH = GLA_HEADS * GLA_DV
DSA_HEADS = 8
DSA_DH = 128
DSA_WIDTH = DSA_HEADS * DSA_DH
IDX_HEADS = 16
IDX_DIM = 64
DSA_TOPK_MAX = 256
DSA_QBLOCK = 64
MIX_WIDTH = GLA_WIDTH + DSA_WIDTH
PEER_HEADS = 8
PEER_NKEYS = 128
PEER_EXPERTS = PEER_NKEYS * PEER_NKEYS
PEER_DQ = 256
PEER_TOPK = 16
PEER_BLOCK = 128
RMS_EPS = 1e-6

SPLIT_SIZES = (GLA_HEADS * GLA_DK, GLA_HEADS * GLA_DK, GLA_WIDTH, GLA_WIDTH, GLA_LR,
               DSA_WIDTH, DSA_WIDTH, DSA_WIDTH, IDX_HEADS * IDX_DIM, IDX_DIM, IDX_HEADS)
IN_WIDTH = sum(SPLIT_SIZES)

kernel_name = 'hymba_gla_dsa_peer_adaln_step'

F32 = jnp.float32


def rmsnorm(x, g):
    xf = x.astype(F32)
    y = xf * lax.rsqrt(jnp.mean(xf * xf, axis=-1, keepdims=True) + RMS_EPS)
    return (y * g.astype(F32)).astype(x.dtype)


def split_proj(p):
    out, o = [], 0
    for s in SPLIT_SIZES:
        out.append(p[..., o:o + s])
        o += s
    return out


def gla_scan(q, k, v, log_a, s0):
    B, T = q.shape[:2]
    C = min(GLA_CHUNK, T)
    nc = -(-T // C)
    pad = nc * C - T

    def chunks(a):
        a = jnp.pad(a, ((0, 0), (0, pad), (0, 0), (0, 0)))
        return a.reshape(B, nc, C, a.shape[2], a.shape[3]).transpose(1, 0, 3, 2, 4)

    tril = jnp.tril(jnp.ones((C, C), dtype=bool))

    def step(s, inp):
        qc, kc, vc, ac = inp
        b = jnp.cumsum(ac, axis=2)
        inter = jnp.einsum('bhcd,bhde->bhce', qc * jnp.exp(b), s)
        diff = jnp.where(tril[:, :, None], b[:, :, :, None, :] - b[:, :, None, :, :], -jnp.inf)
        att = jnp.einsum('bhid,bhjd,bhijd->bhij', qc, kc, jnp.exp(diff))
        intra = jnp.einsum('bhij,bhje->bhie', att, vc)
        bl = b[:, :, -1:, :]
        s = s * jnp.exp(bl[:, :, 0, :])[..., None] + jnp.einsum('bhcd,bhce->bhde', kc * jnp.exp(bl - b), vc)
        return s, inter + intra

    s_fin, o = lax.scan(step, s0.astype(F32), (chunks(q), chunks(k), chunks(v), chunks(log_a)))
    o = o.transpose(1, 0, 3, 2, 4).reshape(B, nc * C, GLA_HEADS, GLA_DV)[:, :T]
    return o, s_fin


def gla_mixer(gq, gk, gv, gr, glr, w_a2, b_a, g_gla, s0):
    B, T = gq.shape[:2]
    q = gq.reshape(B, T, GLA_HEADS, GLA_DK).astype(F32) * (GLA_DK ** -0.5)
    k = gk.reshape(B, T, GLA_HEADS, GLA_DK).astype(F32)
    v = gv.reshape(B, T, GLA_HEADS, GLA_DV).astype(F32)
    z = (glr @ w_a2 + b_a).astype(F32).reshape(B, T, GLA_HEADS, GLA_DK)
    log_a = jax.nn.log_sigmoid(z) / GLA_TAU
    o, s_fin = gla_scan(q, k, v, log_a, s0)
    o = o * lax.rsqrt(jnp.mean(o * o, axis=-1, keepdims=True) + RMS_EPS)
    o = o.reshape(B, T, GLA_WIDTH) * g_gla.astype(F32)
    return o * jax.nn.silu(gr.astype(F32)), s_fin


def indexer_select(qi, w, ki, q_pos, ksel):
    L = ki.shape[1]
    sc = jnp.einsum('bqhd,bsd->bqhs', qi.astype(F32), ki.astype(F32)) * (IDX_DIM ** -0.5)
    score = jnp.einsum('bqh,bqhs->bqs', w.astype(F32), jax.nn.relu(sc))
    kpos = jnp.arange(L)
    score = jnp.where(kpos[None, None, :] <= q_pos[None, :, None], score, -jnp.inf)
    _, idx = lax.top_k(score, ksel)
    valid = idx <= q_pos[None, :, None]
    return idx, valid


def sparse_attend(q, kg, vg, valid):
    s = jnp.einsum('bqhd,bqkhd->bqhk', q.astype(F32), kg.astype(F32)) * (DSA_DH ** -0.5)
    s = jnp.where(valid[:, :, None, :], s, -jnp.inf)
    p = jax.nn.softmax(s, axis=-1)
    return jnp.einsum('bqhk,bqkhd->bqhd', p, vg.astype(F32))


def dsa_prompt(q, k, v, qi, ki, w):
    B, T = q.shape[:2]
    ksel = min(DSA_TOPK_MAX, T // 4)
    qb = min(DSA_QBLOCK, T)
    nb = T // qb
    take = jax.vmap(lambda rows, ids: rows[ids])

    def block(i):
        start = i * qb
        sl = lambda a: lax.dynamic_slice_in_dim(a, start, qb, axis=1)
        q_pos = start + jnp.arange(qb)
        idx, valid = indexer_select(sl(qi), sl(w), ki, q_pos, ksel)
        return sparse_attend(sl(q), take(k, idx), take(v, idx), valid)

    o = lax.map(block, jnp.arange(nb))
    return o.transpose(1, 0, 2, 3, 4).reshape(B, T, DSA_WIDTH)


def dsa_sample(q, k, v, qi, ki, w, cache_k, cache_v, cache_kidx, page_table, layer):
    DB, Tn = q.shape[:2]
    n_pages = page_table.shape[1]
    page = cache_k.shape[2]
    past = n_pages * page
    ki_past = cache_kidx[layer, page_table].reshape(DB, past, IDX_DIM)
    ki_all = jnp.concatenate([ki_past.astype(F32), ki.astype(F32)], axis=1)
    L = past + Tn
    ksel = min(DSA_TOPK_MAX, L // 4)
    q_pos = past + jnp.arange(Tn)
    idx, valid = indexer_select(qi, w, ki_all, q_pos, ksel)

    def one(args):
        qs, ks, vs, pt, ids, vd = args
        in_past = (ids < past)[..., None, None]
        pid = jnp.minimum(ids, past - 1)
        phys = pt[pid // page]
        off = pid % page
        nid = jnp.clip(ids - past, 0, Tn - 1)
        kg = jnp.where(in_past, cache_k[layer, phys, off].astype(F32), ks[nid].astype(F32))
        vg = jnp.where(in_past, cache_v[layer, phys, off].astype(F32), vs[nid].astype(F32))
        return sparse_attend(qs[None], kg[None], vg[None], vd[None])[0]

    o = lax.map(one, (q, k, v, page_table, idx, valid))
    return o.reshape(DB, Tn, DSA_WIDTH)


def peer(h, w_pq, sub_keys, peer_u, peer_v, layer):
    shp = h.shape
    xt = h.reshape(-1, D_MODEL)
    n = xt.shape[0]
    qh = (xt @ w_pq).reshape(n, PEER_HEADS, 2, PEER_DQ // 2).astype(F32)
    s = jnp.einsum('nhpd,pkd->nhpk', qh, sub_keys.astype(F32))
    s1, i1 = lax.top_k(s[:, :, 0], PEER_TOPK)
    s2, i2 = lax.top_k(s[:, :, 1], PEER_TOPK)
    cand = (s1[..., :, None] + s2[..., None, :]).reshape(n, PEER_HEADS, PEER_TOPK * PEER_TOPK)
    cidx = (i1[..., :, None] * PEER_NKEYS + i2[..., None, :]).reshape(n, PEER_HEADS, PEER_TOPK * PEER_TOPK)
    top_s, pos = lax.top_k(cand, PEER_TOPK)
    eidx = jnp.take_along_axis(cidx, pos, axis=-1)
    gate = jax.nn.softmax(top_s, axis=-1)
    blk = min(PEER_BLOCK, n)
    nb = -(-n // blk)
    pad = nb * blk - n
    xb = jnp.pad(xt, ((0, pad), (0, 0))).reshape(nb, blk, D_MODEL)
    eb = jnp.pad(eidx, ((0, pad), (0, 0), (0, 0))).reshape(nb, blk, PEER_HEADS, PEER_TOPK)
    gb = jnp.pad(gate, ((0, pad), (0, 0), (0, 0))).reshape(nb, blk, PEER_HEADS, PEER_TOPK)

    def one(args):
        xs, es, gs = args
        a = jnp.einsum('nd,nhkd->nhk', xs.astype(F32), peer_u[layer, es].astype(F32))
        coef = gs * jax.nn.gelu(a)
        return jnp.einsum('nhk,nhkd->nd', coef, peer_v[layer, es].astype(F32))

    out = lax.map(one, (xb, eb, gb)).reshape(nb * blk, D_MODEL)[:n]
    return out.reshape(shp).astype(h.dtype)


def setup_inputs(seed: int = 0) -> dict:
    key = jax.random.key(seed)
    ks = jax.random.split(key, 24)
    n_pages = PAST_LEN // PAGE_SIZE
    n_pool = (5 * DEC_BATCH * n_pages + 3) // 4
    nrm = jax.random.normal
    page_table = jax.random.permutation(ks[9], n_pool)[:DEC_BATCH * n_pages].reshape(DEC_BATCH, n_pages).astype(jnp.int32)
    return {
        'x_prompt': nrm(ks[0], (BATCH, SEQ, D_MODEL), F32),
        'x_sample': nrm(ks[1], (DEC_BATCH, DEC_SEQ, D_MODEL), F32),
        'c_prompt': nrm(ks[2], (BATCH, D_MODEL), F32),
        'c_sample': nrm(ks[3], (DEC_BATCH, D_MODEL), F32),
        'cache_k': nrm(ks[4], (DEPTH, n_pool, PAGE_SIZE, DSA_HEADS, DSA_DH), F32),
        'cache_v': nrm(ks[5], (DEPTH, n_pool, PAGE_SIZE, DSA_HEADS, DSA_DH), F32),
        'cache_kidx': nrm(ks[6], (DEPTH, n_pool, PAGE_SIZE, IDX_DIM), F32),
        'state_gla': nrm(ks[7], (DEPTH, DEC_BATCH, GLA_HEADS, GLA_DK, GLA_DV), F32) * 0.5,
        'page_table': page_table,
        'w_ada': nrm(ks[10], (DEPTH, D_MODEL, 6 * D_MODEL), F32) * (0.3 * D_MODEL ** -0.5),
        'b_ada': nrm(ks[11], (DEPTH, 6 * D_MODEL), F32) * 0.01,
        'g_mix': 1.0 + 0.01 * nrm(ks[12], (DEPTH, D_MODEL), F32),
        'g_ffn': 1.0 + 0.01 * nrm(ks[13], (DEPTH, D_MODEL), F32),
        'w_in': nrm(ks[14], (DEPTH, D_MODEL, IN_WIDTH), F32) * (D_MODEL ** -0.5),
        'w_a2': nrm(ks[15], (DEPTH, GLA_LR, GLA_HEADS * GLA_DK), F32) * (GLA_LR ** -0.5),
        'b_a': nrm(ks[16], (DEPTH, GLA_HEADS * GLA_DK), F32) * 0.01,
        'g_gla': 1.0 + 0.01 * nrm(ks[17], (DEPTH, GLA_WIDTH), F32),
        'w_out': nrm(ks[18], (DEPTH, MIX_WIDTH, D_MODEL), F32) * (MIX_WIDTH ** -0.5),
        'peer_wq': nrm(ks[19], (DEPTH, D_MODEL, PEER_HEADS * PEER_DQ), F32) * (D_MODEL ** -0.5),
        'peer_keys': nrm(ks[20], (DEPTH, 2, PEER_NKEYS, PEER_DQ // 2), F32) * ((PEER_DQ // 2) ** -0.5),
        'peer_u': nrm(ks[21], (DEPTH, PEER_EXPERTS, D_MODEL), F32) * (D_MODEL ** -0.5),
        'peer_v': nrm(ks[22], (DEPTH, PEER_EXPERTS, D_MODEL), F32) * (PEER_HEADS ** -0.5),
        'g_final': 1.0 + 0.01 * nrm(ks[23], (D_MODEL,), F32),
    }


def reference(x_prompt, x_sample, c_prompt, c_sample, cache_k, cache_v, cache_kidx, state_gla, page_table,
              w_ada, b_ada, g_mix, g_ffn, w_in, w_a2, b_a, g_gla, w_out, peer_wq, peer_keys, peer_u, peer_v,
              g_final):
    def mixer(h, l, s0, dsa_fn):
        B, T = h.shape[:2]
        gq, gk, gv, gr, glr, dq, dk, dv, dqi, dki, dw = split_proj(h @ w_in[l])
        y_a, s_new = gla_mixer(gq, gk, gv, gr, glr, w_a2[l], b_a[l], g_gla[l], s0)
        q_b = dq.reshape(B, T, DSA_HEADS, DSA_DH)
        k_b = dk.reshape(B, T, DSA_HEADS, DSA_DH)
        v_b = dv.reshape(B, T, DSA_HEADS, DSA_DH)
        qi = dqi.reshape(B, T, IDX_HEADS, IDX_DIM)
        wi = dw * (IDX_HEADS ** -0.5)
        y_b = dsa_fn(q_b, k_b, v_b, qi, dki, wi, l)
        m = jnp.concatenate([y_a, y_b], axis=-1).astype(h.dtype)
        return m, (k_b, v_b, dki, s_new.astype(state_gla.dtype))

    def dsa_p(q, k, v, qi, ki, wi, l):
        return dsa_prompt(q, k, v, qi, ki, wi)

    def dsa_s(q, k, v, qi, ki, wi, l):
        return dsa_sample(q, k, v, qi, ki, wi, cache_k, cache_v, cache_kidx, page_table, l)

    def run(x, c, s0_fn, dsa_fn):
        news = []
        for l in range(DEPTH):
            mod = (c @ w_ada[l] + b_ada[l])[:, None, :]
            sh1, sc1, gt1, sh2, sc2, gt2 = jnp.split(mod, 6, axis=-1)
            h = rmsnorm(x, g_mix[l]) * (1.0 + sc1) + sh1
            m, st = mixer(h, l, s0_fn(l), dsa_fn)
            x = x + gt1 * (m @ w_out[l])
            h = rmsnorm(x, g_ffn[l]) * (1.0 + sc2) + sh2
            x = x + gt2 * peer(h, peer_wq[l], peer_keys[l], peer_u, peer_v, l)
            news.append(st)
        y = rmsnorm(x, g_final)
        stacked = [jnp.stack([s[i] for s in news]) for i in range(4)]
        return y, stacked

    zero_state = jnp.zeros((BATCH, GLA_HEADS, GLA_DK, GLA_DV), state_gla.dtype)
    y_prompt, sp = run(x_prompt, c_prompt, lambda l: zero_state, dsa_p)
    y_sample, ss = run(x_sample, c_sample, lambda l: state_gla[l], dsa_s)
    new_k_prompt, new_v_prompt, new_kidx_prompt, new_gla_prompt = sp
    new_k_sample, new_v_sample, new_kidx_sample, new_gla_sample = ss
    return (y_prompt, y_sample, new_k_prompt, new_v_prompt, new_kidx_prompt, new_gla_prompt,
            new_k_sample, new_v_sample, new_kidx_sample, new_gla_sample)
```

```python
import functools

import jax
import jax.numpy as jnp
from jax import lax
from jax.experimental import pallas as pl
from jax.experimental.pallas import tpu as pltpu

F32 = jnp.float32
BF16 = jnp.bfloat16
HIGHEST = lax.Precision.HIGHEST
NEG_INF = float("-inf")

D_MODEL = 2048
DEPTH = 2
PAGE_SIZE = 128
GLA_HEADS, GLA_DK, GLA_DV, GLA_LR, GLA_TAU, GLA_CHUNK = 4, 128, 256, 16, 16.0, 64
GLA_SUBCHUNK = 16
GLA_EXP_CLAMP = 80.0
DSA_HEADS, DSA_DH, IDX_HEADS, IDX_DIM, DSA_TOPK_MAX = 8, 128, 16, 64, 256
PEER_HEADS, PEER_NKEYS, PEER_DQ, PEER_TOPK = 8, 128, 256, 16
PEER_EXPERTS = PEER_NKEYS * PEER_NKEYS
RMS_EPS = 1e-6
BISECT_ITERS = 32

SPLIT_SIZES = (512, 512, 1024, 1024, 16, 1024, 1024, 1024, 1024, 64, 16)
COL_GQ, COL_GK, COL_GV, COL_GR = 0, 512, 1024, 2048
COL_DQ, COL_DK, COL_DV, COL_DQI, COL_TAIL = 3072, 4096, 5120, 6144, 7168
TAIL_KI, TAIL_LR, TAIL_W = 0, 64, 80
NP_IN = 7296
MOD_ROWS = 192
MOD_PROMPT_ROW0 = 128
(MOD_SH1, MOD_SC1, MOD_GT1, MOD_SH2, MOD_SC2, MOD_GT2) = range(6)

VMEM_LIMIT = 56 * 1024 * 1024


def _cparams(sem):
    return pltpu.CompilerParams(dimension_semantics=sem, vmem_limit_bytes=VMEM_LIMIT)


def _dot_t(a, b):
    return lax.dot_general(a, b, (((1,), (1,)), ((), ())), preferred_element_type=F32)


def _dot_lt(a, b):
    return lax.dot_general(a, b, (((0,), (0,)), ((), ())), preferred_element_type=F32)


def _row_to_col(row):
    return jnp.transpose(jnp.broadcast_to(row, (128, 128)))[:, :1]


def _ada_kernel(c_ref, w_ref, b_ref, o_ref):
    o_ref[0] = jnp.dot(c_ref[...].astype(BF16), w_ref[0].astype(BF16), preferred_element_type=F32) + b_ref[0]


def ada_mod(c_all, w_ada, b_ada, tn=1024):
    m, d = c_all.shape
    n = w_ada.shape[-1]
    return pl.pallas_call(
        _ada_kernel,
        grid=(DEPTH, n // tn),
        in_specs=[
            pl.BlockSpec((m, d), lambda l, j: (0, 0)),
            pl.BlockSpec((1, d, tn), lambda l, j: (l, 0, j)),
            pl.BlockSpec((1, 1, tn), lambda l, j: (l, 0, j)),
        ],
        out_specs=pl.BlockSpec((1, m, tn), lambda l, j: (l, 0, j)),
        out_shape=jax.ShapeDtypeStruct((DEPTH, m, n), F32),
        compiler_params=_cparams(("parallel", "parallel")),
        name="ada_mod",
    )(c_all, w_ada, b_ada.reshape(DEPTH, 1, n))


def _norm_mod(x, g, sc, sh):
    ms = jnp.mean(x * x, axis=-1, keepdims=True)
    y = x * lax.rsqrt(ms + RMS_EPS) * g
    return y * (1.0 + sc) + sh


def _norm_proj_kernel(x_ref, g_ref, sc_ref, sh_ref, w_ref, o_ref):
    h = _norm_mod(x_ref[...], g_ref[...], sc_ref[...], sh_ref[...])
    tg, ts, d = h.shape
    o_ref[...] = jnp.dot(h.reshape(tg * ts, d).astype(BF16), w_ref[0], preferred_element_type=F32)


def _x_specs(g, s, tg, ts, mod_row0, pieces, layer, grid_rank_prefix):
    ns = s // ts
    assert mod_row0 % tg == 0

    def tok(*idx):
        return idx[grid_rank_prefix]

    x_spec = pl.BlockSpec((tg, ts, D_MODEL), lambda *idx: (tok(*idx) // ns, tok(*idx) % ns, 0))
    g_spec = pl.BlockSpec((1, 1, D_MODEL), lambda *idx: (layer, 0, 0))
    mod_specs = [
        pl.BlockSpec((tg, 1, D_MODEL), lambda *idx, p=p: (mod_row0 // tg + tok(*idx) // ns, 0, p))
        for p in pieces
    ]
    return x_spec, g_spec, mod_specs, (g // tg) * ns


def norm_proj(x3, gain3, mod3, mod_row0, layer, w, tg, ts, tn):
    g, s, d = x3.shape
    n = w.shape[-1]
    x_spec, g_spec, mod_specs, mt = _x_specs(g, s, tg, ts, mod_row0, (MOD_SC1, MOD_SH1), layer, 1)
    return pl.pallas_call(
        _norm_proj_kernel,
        grid=(n // tn, mt),
        in_specs=[x_spec, g_spec, *mod_specs, pl.BlockSpec((1, d, tn), lambda j, i: (layer, 0, j))],
        out_specs=pl.BlockSpec((tg * ts, tn), lambda j, i: (i, j)),
        out_shape=jax.ShapeDtypeStruct((g * s, n), F32),
        compiler_params=_cparams(("parallel", "parallel")),
        name="norm_proj",
    )(x3, gain3, mod3, mod3, w)


def _norm_proj_t_kernel(x_ref, g_ref, sc_ref, sh_ref, wt_ref, qt_ref, h_ref):
    h = _norm_mod(x_ref[...], g_ref[...], sc_ref[...], sh_ref[...])
    tg, ts, d = h.shape
    hb = h.reshape(tg * ts, d).astype(BF16)
    h_ref[...] = hb
    qt_ref[...] = _dot_t(wt_ref[0], hb)


def norm_proj_t(x3, gain3, mod3, mod_row0, layer, wt, tg, ts):
    g, s, d = x3.shape
    n = wt.shape[1]
    x_spec, g_spec, mod_specs, mt = _x_specs(g, s, tg, ts, mod_row0, (MOD_SC2, MOD_SH2), layer, 0)
    tm = tg * ts
    return pl.pallas_call(
        _norm_proj_t_kernel,
        grid=(mt,),
        in_specs=[x_spec, g_spec, *mod_specs, pl.BlockSpec((1, n, d), lambda i: (layer, 0, 0))],
        out_specs=[pl.BlockSpec((n, tm), lambda i: (0, i)), pl.BlockSpec((tm, d), lambda i: (i, 0))],
        out_shape=[jax.ShapeDtypeStruct((n, g * s), F32), jax.ShapeDtypeStruct((g * s, d), BF16)],
        compiler_params=_cparams(("parallel",)),
        name="norm_proj_t",
    )(x3, gain3, mod3, mod3, wt)


def _log_sigmoid(z):
    return jnp.minimum(z, 0.0) - jnp.log(1.0 + jnp.exp(-jnp.abs(z)))


def _gla_kernel(q_ref, k_ref, v_ref, r_ref, tail_ref, wa_ref, ba_ref, gg_ref, s0_ref, y_ref, sout_ref, s_ref,
                *, chunk, sub, n_chunks):
    t = pl.program_id(1)

    @pl.when(t == 0)
    def _():
        s_ref[...] = s0_ref[0]

    row = lax.broadcasted_iota(jnp.int32, (chunk, chunk), 0)
    col = lax.broadcasted_iota(jnp.int32, (chunk, chunk), 1)
    tri = row >= col
    tri_f = tri.astype(F32)
    n_sub = chunk // sub

    def do_chunk(c):
        rows = pl.ds(pl.multiple_of(c * chunk, chunk), chunk)
        z = jnp.dot(tail_ref[0, rows, :], wa_ref[...], precision=HIGHEST, preferred_element_type=F32) + ba_ref[...]
        log_a = _log_sigmoid(z) * (1.0 / GLA_TAU)
        bcum = jnp.dot(tri_f, log_a, precision=HIGHEST, preferred_element_type=F32)
        for h in range(GLA_HEADS):
            ksl = slice(h * GLA_DK, (h + 1) * GLA_DK)
            vsl = slice(h * GLA_DV, (h + 1) * GLA_DV)
            q = q_ref[0, rows, ksl] * (GLA_DK ** -0.5)
            k = k_ref[0, rows, ksl]
            v = v_ref[0, rows, vsl]
            b = bcum[:, ksl]
            state = s_ref[h]
            inter = jnp.dot(q * jnp.exp(b), state, preferred_element_type=F32)
            att_rows = []
            for s in range(n_sub):
                ref_b = b[s * sub - 1:s * sub, :] if s > 0 else jnp.zeros((1, GLA_DK), F32)
                qs = q[s * sub:(s + 1) * sub, :] * jnp.exp(b[s * sub:(s + 1) * sub, :] - ref_b)
                ks = k * jnp.exp(jnp.minimum(ref_b - b, GLA_EXP_CLAMP))
                att_rows.append(_dot_t(qs, ks))
            att = att_rows[0] if n_sub == 1 else jnp.concatenate(att_rows, axis=0)
            att = jnp.where(tri, att, 0.0)
            o = inter + jnp.dot(att, v, preferred_element_type=F32)
            b_last = b[chunk - 1:chunk, :]
            decay_col = _row_to_col(jnp.exp(b_last))
            s_ref[h] = state * decay_col + _dot_lt(k * jnp.exp(b_last - b), v)
            o = o * lax.rsqrt(jnp.mean(o * o, axis=-1, keepdims=True) + RMS_EPS)
            y_ref[0, rows, vsl] = o * gg_ref[:, vsl] * jax.nn.silu(r_ref[0, rows, vsl])

    if n_chunks == 1:
        do_chunk(0)
    else:
        def body(c, carry):
            do_chunk(c)
            return carry
        lax.fori_loop(0, n_chunks, body, 0)

    @pl.when(t == pl.num_programs(1) - 1)
    def _():
        sout_ref[0] = s_ref[...]


def gla_mixer(p3, wa_pad, ba, gg, s0, s0_row0, tt):
    b, t, _ = p3.shape
    chunk = min(GLA_CHUNK, t)
    assert t % tt == 0 and tt % chunk == 0
    sub = min(GLA_SUBCHUNK, chunk)
    kern = functools.partial(_gla_kernel, chunk=chunk, sub=sub, n_chunks=tt // chunk)
    kw = GLA_HEADS * GLA_DK
    vw = GLA_HEADS * GLA_DV
    return pl.pallas_call(
        kern,
        grid=(b, t // tt),
        in_specs=[
            pl.BlockSpec((1, tt, kw), lambda i, j: (i, j, COL_GQ // kw)),
            pl.BlockSpec((1, tt, kw), lambda i, j: (i, j, COL_GK // kw)),
            pl.BlockSpec((1, tt, vw), lambda i, j: (i, j, COL_GV // vw)),
            pl.BlockSpec((1, tt, vw), lambda i, j: (i, j, COL_GR // vw)),
            pl.BlockSpec((1, tt, 128), lambda i, j: (i, j, COL_TAIL // 128)),
            pl.BlockSpec((128, kw), lambda i, j: (0, 0)),
            pl.BlockSpec((1, kw), lambda i, j: (0, 0)),
            pl.BlockSpec((1, vw), lambda i, j: (0, 0)),
            pl.BlockSpec((1, GLA_HEADS, GLA_DK, GLA_DV), lambda i, j: (s0_row0 + i, 0, 0, 0)),
        ],
        out_specs=[
            pl.BlockSpec((1, tt, vw), lambda i, j: (i, j, 0)),
            pl.BlockSpec((1, GLA_HEADS, GLA_DK, GLA_DV), lambda i, j: (i, 0, 0, 0)),
        ],
        out_shape=[jax.ShapeDtypeStruct((b, t, vw), F32),
                   jax.ShapeDtypeStruct((b, GLA_HEADS, GLA_DK, GLA_DV), F32)],
        scratch_shapes=[pltpu.VMEM((GLA_HEADS, GLA_DK, GLA_DV), F32)],
        compiler_params=_cparams(("parallel", "arbitrary")),
        name="gla_mixer",
    )(p3, p3, p3, p3, p3, wa_pad, ba, gg, s0)


def _bisect_threshold(read_scores, lo, hi, k, axis):
    def body(_, c):
        lo, hi = c
        mid = 0.5 * lo + 0.5 * hi
        cnt = jnp.sum(jnp.where(read_scores() >= mid, 1.0, 0.0), axis=axis, keepdims=True)
        ok = cnt >= float(k)
        return jnp.where(ok, mid, lo), jnp.where(ok, hi, mid)
    lo, hi = lax.fori_loop(0, BISECT_ITERS, body, (lo, hi))
    return lo


def _dsa_prompt_kernel(q_ref, qi_ref, tail_ref, k_ref, v_ref, ki_ref, o_ref, sc_ref, *, tq, tk, ksel):
    t_keys = k_ref.shape[1]
    qpos = pl.program_id(1) * tq + lax.broadcasted_iota(jnp.int32, (tq, 1), 0)
    kpos = lax.broadcasted_iota(jnp.int32, (1, t_keys), 1)
    causal = kpos <= qpos
    w = tail_ref[0, :, TAIL_W:TAIL_W + IDX_HEADS] * ((IDX_HEADS ** -0.5) * (IDX_DIM ** -0.5))

    for kb in range(t_keys // tk):
        ki = ki_ref[0, kb * tk:(kb + 1) * tk, :]
        acc = jnp.zeros((tq, tk), F32)
        for h in range(IDX_HEADS):
            sc = _dot_t(qi_ref[0, :, h * IDX_DIM:(h + 1) * IDX_DIM], ki)
            acc = acc + w[:, h:h + 1] * jnp.maximum(sc, 0.0)
        sc_ref[:, kb * tk:(kb + 1) * tk] = acc
    score = jnp.where(causal, sc_ref[...], NEG_INF)
    sc_ref[...] = score
    hi = jnp.max(score, axis=1, keepdims=True)
    lo = jnp.min(jnp.where(causal, score, jnp.inf), axis=1, keepdims=True)
    lo = _bisect_threshold(lambda: sc_ref[...], lo, hi, ksel, 1)
    thr = jnp.where(qpos + 1 <= ksel, NEG_INF, lo)
    sc_ref[...] = jnp.where((sc_ref[...] >= thr) & causal, 0.0, NEG_INF)

    for h in range(DSA_HEADS):
        sl = slice(h * DSA_DH, (h + 1) * DSA_DH)
        s = _dot_t(q_ref[0, :, sl], k_ref[0, :, sl]) * (DSA_DH ** -0.5) + sc_ref[...]
        p = jnp.exp(s - jnp.max(s, axis=1, keepdims=True))
        denom = jnp.sum(p, axis=1, keepdims=True)
        o_ref[0, :, sl] = jnp.dot(p.astype(BF16), v_ref[0, :, sl], preferred_element_type=F32) / denom


def dsa_prompt(q16, qi16, p3, k16, v16, ki16, tq=256, tk=512):
    b, t, _ = q16.shape
    ksel = min(DSA_TOPK_MAX, t // 4)
    w = DSA_HEADS * DSA_DH
    kern = functools.partial(_dsa_prompt_kernel, tq=tq, tk=min(tk, t), ksel=ksel)
    return pl.pallas_call(
        kern,
        grid=(b, t // tq),
        in_specs=[
            pl.BlockSpec((1, tq, w), lambda i, j: (i, j, 0)),
            pl.BlockSpec((1, tq, IDX_HEADS * IDX_DIM), lambda i, j: (i, j, 0)),
            pl.BlockSpec((1, tq, 128), lambda i, j: (i, j, COL_TAIL // 128)),
            pl.BlockSpec((1, t, w), lambda i, j: (i, 0, 0)),
            pl.BlockSpec((1, t, w), lambda i, j: (i, 0, 0)),
            pl.BlockSpec((1, t, IDX_DIM), lambda i, j: (i, 0, 0)),
        ],
        out_specs=pl.BlockSpec((1, tq, w), lambda i, j: (i, j, 0)),
        out_shape=jax.ShapeDtypeStruct((b, t, w), F32),
        scratch_shapes=[pltpu.VMEM((tq, t), F32)],
        compiler_params=_cparams(("parallel", "parallel")),
        name="dsa_prompt",
    )(q16, qi16, p3, k16, v16, ki16)


def _group_sum_lanes(x, group):
    shift = x.shape[-1] // 2
    while shift >= group:
        x = x + pltpu.roll(x, shift, x.ndim - 1)
        shift //= 2
    return x


def _dsa_sample_kernel(pt_ref, q_ref, qi_ref, tail_ref, kn_ref, vn_ref, *refs, n_pages, ksel):
    del pt_ref
    kidx_refs = refs[:n_pages]
    k_refs = refs[n_pages:2 * n_pages]
    v_refs = refs[2 * n_pages:3 * n_pages]
    o_ref, st_ref, att_ref = refs[3 * n_pages:]
    tn = q_ref.shape[1]
    page = PAGE_SIZE
    past = n_pages * page
    ncol = 128
    assert IDX_HEADS * tn == ncol

    tail = tail_ref[0]
    w = tail[:, TAIL_W:TAIL_W + IDX_HEADS] * ((IDX_HEADS ** -0.5) * (IDX_DIM ** -0.5))
    qi = qi_ref[0]
    qs = jnp.concatenate([qi[:, h * IDX_DIM:(h + 1) * IDX_DIM] for h in range(IDX_HEADS)], axis=0).astype(BF16)
    wcol = jnp.concatenate([w[:, h:h + 1] for h in range(IDX_HEADS)], axis=0)
    wmat = jnp.transpose(jnp.broadcast_to(wcol, (ncol, ncol)))

    def idx_scores(ki):
        sc = _dot_t(ki, qs)
        return _group_sum_lanes(wmat[:ki.shape[0]] * jnp.maximum(sc, 0.0), tn)

    for p in range(n_pages):
        st_ref[p * page:(p + 1) * page, :] = idx_scores(kidx_refs[p][0].astype(BF16))
    lane_t = lax.broadcasted_iota(jnp.int32, (tn, ncol), 1) % tn
    new_row = lax.broadcasted_iota(jnp.int32, (tn, ncol), 0)
    new_valid = new_row <= lane_t
    st_ref[past:past + tn, :] = jnp.where(
        new_valid, idx_scores(tail[:, TAIL_KI:TAIL_KI + IDX_DIM].astype(BF16)), NEG_INF)

    score = st_ref[...]
    hi = jnp.max(score, axis=0, keepdims=True)
    lo = jnp.min(jnp.where(score == NEG_INF, jnp.inf, score), axis=0, keepdims=True)
    thr = _bisect_threshold(lambda: st_ref[...], lo, hi, ksel, 0)
    st_ref[...] = jnp.where(st_ref[...] >= thr, 0.0, NEG_INF)

    q = q_ref[0]
    q_rep = jnp.concatenate([q] * (ncol // tn), axis=0)
    q_t = jnp.transpose(q_rep)
    row_head = lax.broadcasted_iota(jnp.int32, (DSA_HEADS * DSA_DH, ncol), 0) // DSA_DH
    col_grp = lax.broadcasted_iota(jnp.int32, (DSA_HEADS * DSA_DH, ncol), 1) // tn
    q_bd = jnp.where(row_head == col_grp, q_t, 0.0).astype(BF16)

    scale = DSA_DH ** -0.5
    for p in range(n_pages):
        rows = slice(p * page, (p + 1) * page)
        att_ref[rows, :] = jnp.dot(k_refs[p][0].astype(BF16), q_bd, preferred_element_type=F32) * scale + st_ref[rows, :]
    rows_new = slice(past, past + tn)
    att_ref[rows_new, :] = jnp.dot(kn_ref[0].astype(BF16), q_bd, preferred_element_type=F32) * scale + st_ref[rows_new, :]

    s_all = att_ref[...]
    m = jnp.max(s_all, axis=0, keepdims=True)
    att_ref[...] = jnp.exp(s_all - m)
    denom = jnp.sum(att_ref[...], axis=0, keepdims=True)
    acc = _dot_lt(att_ref[rows_new, :].astype(BF16), vn_ref[0].astype(BF16))
    for p in range(n_pages):
        rows = slice(p * page, (p + 1) * page)
        acc = acc + _dot_lt(att_ref[rows, :].astype(BF16), v_refs[p][0].astype(BF16))
    inv = _row_to_col(1.0 / denom)
    acc = acc * inv
    for h in range(DSA_HEADS):
        o_ref[0, :, h * DSA_DH:(h + 1) * DSA_DH] = acc[h * tn:(h + 1) * tn, h * DSA_DH:(h + 1) * DSA_DH]


def dsa_sample(p3, page_rows, cache_k2, cache_v2, cache_kidx2):
    db, tn, _ = p3.shape
    n_pages = page_rows.shape[0] // db
    length = n_pages * PAGE_SIZE + tn
    ksel = min(DSA_TOPK_MAX, length // 4)
    w = DSA_HEADS * DSA_DH

    def page_spec(width, p):
        return pl.BlockSpec((1, PAGE_SIZE, width), lambda i, pt, p=p: (pt[i * n_pages + p], 0, 0))

    in_specs = [
        pl.BlockSpec((1, tn, w), lambda i, pt: (i, 0, COL_DQ // w)),
        pl.BlockSpec((1, tn, w), lambda i, pt: (i, 0, COL_DQI // w)),
        pl.BlockSpec((1, tn, 128), lambda i, pt: (i, 0, COL_TAIL // 128)),
        pl.BlockSpec((1, tn, w), lambda i, pt: (i, 0, COL_DK // w)),
        pl.BlockSpec((1, tn, w), lambda i, pt: (i, 0, COL_DV // w)),
    ]
    in_specs += [page_spec(IDX_DIM, p) for p in range(n_pages)]
    in_specs += [page_spec(w, p) for p in range(n_pages)]
    in_specs += [page_spec(w, p) for p in range(n_pages)]
    kern = functools.partial(_dsa_sample_kernel, n_pages=n_pages, ksel=ksel)
    return pl.pallas_call(
        kern,
        grid_spec=pltpu.PrefetchScalarGridSpec(
            num_scalar_prefetch=1,
            grid=(db,),
            in_specs=in_specs,
            out_specs=pl.BlockSpec((1, tn, w), lambda i, pt: (i, 0, 0)),
            scratch_shapes=[pltpu.VMEM((length, 128), F32), pltpu.VMEM((length, 128), F32)],
        ),
        out_shape=jax.ShapeDtypeStruct((db, tn, w), F32),
        compiler_params=_cparams(("parallel",)),
        name="dsa_sample",
    )(page_rows, p3, p3, p3, p3, p3, *([cache_kidx2] * n_pages), *([cache_k2] * n_pages), *([cache_v2] * n_pages))


def _out_proj_kernel(x_ref, gt_ref, ya_ref, yb_ref, w_ref, o_ref):
    wa = ya_ref.shape[-1]
    m = jnp.dot(ya_ref[...].astype(BF16), w_ref[0, :wa, :], preferred_element_type=F32)
    m = m + jnp.dot(yb_ref[...].astype(BF16), w_ref[0, wa:, :], preferred_element_type=F32)
    x = x_ref[...]
    o_ref[...] = x + gt_ref[...] * m.reshape(x.shape)


def out_proj_residual(x3, mod3, mod_row0, layer, ya, yb, w_out16, tg, ts):
    g, s, d = x3.shape
    x_spec, _, mod_specs, mt = _x_specs(g, s, tg, ts, mod_row0, (MOD_GT1,), layer, 0)
    tm = tg * ts
    wa, wb = ya.shape[-1], yb.shape[-1]
    return pl.pallas_call(
        _out_proj_kernel,
        grid=(mt,),
        in_specs=[x_spec, *mod_specs,
                  pl.BlockSpec((tm, wa), lambda i: (i, 0)),
                  pl.BlockSpec((tm, wb), lambda i: (i, 0)),
                  pl.BlockSpec((1, wa + wb, d), lambda i: (layer, 0, 0))],
        out_specs=x_spec,
        out_shape=jax.ShapeDtypeStruct(x3.shape, F32),
        compiler_params=_cparams(("parallel",)),
        name="out_proj",
    )(x3, mod3, ya, yb, w_out16)


def _extract_top(vals_list, n):
    cur = list(vals_list)
    out = []
    for _ in range(n):
        m = cur[0].max(axis=0, keepdims=True)
        for c in cur[1:]:
            m = jnp.maximum(m, c.max(axis=0, keepdims=True))
        out.append(m)
        cur = [jnp.where(c >= m, NEG_INF, c) for c in cur]
    return out


def _peer_route_kernel(qt_ref, keys_ref, s1_ref, s2_ref, e1_ref, e2_ref, tc_ref):
    half = PEER_DQ // 2
    tops = []
    raw = []
    for p in range(2):
        s = jnp.dot(keys_ref[0, p], qt_ref[p * half:(p + 1) * half, :], precision=HIGHEST,
                    preferred_element_type=F32)
        raw.append(s)
        tops.append(_extract_top([s], PEER_TOPK))
    cand = [tops[0][a] + jnp.concatenate(tops[1], axis=0) for a in range(PEER_TOPK)]
    ctop = _extract_top(cand, PEER_TOPK)
    cmax = ctop[0]
    z = jnp.zeros_like(cmax)
    for c in ctop:
        z = z + jnp.exp(c - cmax)
    for p, (s_ref, e_ref) in enumerate(((s1_ref, e1_ref), (s2_ref, e2_ref))):
        s = raw[p]
        keep = s >= tops[p][PEER_TOPK - 1]
        s_ref[...] = jnp.where(keep, s, NEG_INF)
        e = jnp.where(keep, jnp.exp(s - tops[p][0]), 0.0)
        e_ref[...] = e / z if p == 1 else e
    tc_ref[0] = jnp.broadcast_to(ctop[PEER_TOPK - 1], tc_ref.shape[1:])


def peer_route(qt, keys, layer, tm=256):
    n = qt.shape[1]
    rows = PEER_HEADS * PEER_NKEYS
    blk = pl.BlockSpec((PEER_NKEYS, tm), lambda i, h: (h, i))
    shp = jax.ShapeDtypeStruct((rows, n), F32)
    return pl.pallas_call(
        _peer_route_kernel,
        grid=(n // tm, PEER_HEADS),
        in_specs=[pl.BlockSpec((PEER_DQ, tm), lambda i, h: (h, i)),
                  pl.BlockSpec((1, 2, PEER_NKEYS, PEER_DQ // 2), lambda i, h: (layer, 0, 0, 0))],
        out_specs=[blk, blk, blk, blk, pl.BlockSpec((1, 8, tm), lambda i, h: (h, 0, i))],
        out_shape=[shp, shp, shp, shp, jax.ShapeDtypeStruct((PEER_HEADS, 8, n), F32)],
        compiler_params=_cparams(("parallel", "parallel")),
        name="peer_route",
    )(qt, keys)


def _peer_dense_kernel(x_ref, gt_ref, h_ref, s1_ref, s2_ref, e1_ref, e2_ref, tc_ref, u_ref, v_ref, o_ref, acc_ref,
                       *, eb):
    j = pl.program_id(1)

    @pl.when(j == 0)
    def _():
        acc_ref[...] = jnp.zeros_like(acc_ref)

    n_i1 = eb // PEER_NKEYS
    a_t = _dot_t(u_ref[0], h_ref[...])
    act = jax.nn.gelu(a_t)
    assert n_i1 % 8 == 0
    coef_rows = []
    for r8 in range(n_i1 // 8):
        s1_rows, e1_rows = [], []
        for h in range(PEER_HEADS):
            rows = pl.ds(pl.multiple_of(h * PEER_NKEYS + j * n_i1 + r8 * 8, 8), 8)
            s1_rows.append(s1_ref[rows, :])
            e1_rows.append(e1_ref[rows, :])
        for r in range(8):
            g = jnp.zeros((PEER_NKEYS, a_t.shape[1]), F32)
            for h in range(PEER_HEADS):
                hs = slice(h * PEER_NKEYS, (h + 1) * PEER_NKEYS)
                c = s2_ref[hs, :] + s1_rows[h][r:r + 1, :]
                g = g + jnp.where(c >= tc_ref[h, 0:1, :], e2_ref[hs, :] * e1_rows[h][r:r + 1, :], 0.0)
            i1 = r8 * 8 + r
            coef_rows.append(g * act[i1 * PEER_NKEYS:(i1 + 1) * PEER_NKEYS, :])
    coef_t = coef_rows[0] if n_i1 == 1 else jnp.concatenate(coef_rows, axis=0)
    acc_ref[...] += jnp.dot(jnp.transpose(coef_t).astype(BF16), v_ref[0], preferred_element_type=F32)

    @pl.when(j == pl.num_programs(1) - 1)
    def _():
        x = x_ref[...]
        o_ref[...] = x + gt_ref[...] * acc_ref[...].reshape(x.shape)


def peer_dense(x3, mod3, mod_row0, layer, h16, s1, s2, e1, e2, tc, u16, v16, tg, ts, eb=1024):
    g, s, d = x3.shape
    x_spec, _, mod_specs, mt = _x_specs(g, s, tg, ts, mod_row0, (MOD_GT2,), layer, 0)
    tm = tg * ts
    rows = PEER_HEADS * PEER_NKEYS
    once = pl.Buffered(1)
    tok = pl.BlockSpec((rows, tm), lambda i, j: (0, i), pipeline_mode=once)
    kern = functools.partial(_peer_dense_kernel, eb=eb)
    return pl.pallas_call(
        kern,
        grid=(mt, PEER_EXPERTS // eb),
        in_specs=[x_spec, *mod_specs,
                  pl.BlockSpec((tm, d), lambda i, j: (i, 0), pipeline_mode=once),
                  tok, tok, tok, tok,
                  pl.BlockSpec((PEER_HEADS, 8, tm), lambda i, j: (0, 0, i), pipeline_mode=once),
                  pl.BlockSpec((1, eb, d), lambda i, j: (layer, j, 0)),
                  pl.BlockSpec((1, eb, d), lambda i, j: (layer, j, 0))],
        out_specs=x_spec,
        out_shape=jax.ShapeDtypeStruct(x3.shape, F32),
        scratch_shapes=[pltpu.VMEM((tm, d), F32)],
        compiler_params=_cparams(("parallel", "arbitrary")),
        name="peer_dense",
    )(x3, mod3, h16, s1, s2, e1, e2, tc, u16, v16)


def _final_norm_kernel(x_ref, g_ref, o_ref):
    x = x_ref[...]
    o_ref[...] = x * lax.rsqrt(jnp.mean(x * x, axis=-1, keepdims=True) + RMS_EPS) * g_ref[...]


def final_norm(x3, g_final, tg, ts):
    g, s, d = x3.shape
    ns = s // ts
    spec = pl.BlockSpec((tg, ts, d), lambda i: (i // ns, i % ns, 0))
    return pl.pallas_call(
        _final_norm_kernel,
        grid=((g // tg) * ns,),
        in_specs=[spec, pl.BlockSpec((1, 1, d), lambda i: (0, 0, 0))],
        out_specs=spec,
        out_shape=jax.ShapeDtypeStruct(x3.shape, F32),
        compiler_params=_cparams(("parallel",)),
        name="final_norm",
    )(x3, g_final.reshape(1, 1, d))


def _prep_w_in(w_in):
    o = [0]
    for sz in SPLIT_SIZES:
        o.append(o[-1] + sz)
    pad = jnp.zeros(w_in.shape[:-1] + (NP_IN - o[-1],), w_in.dtype)
    parts = [w_in[..., o[0]:o[4]], w_in[..., o[5]:o[9]], w_in[..., o[9]:o[10]], w_in[..., o[4]:o[5]],
             w_in[..., o[10]:o[11]], pad]
    return jnp.concatenate(parts, axis=-1).astype(BF16)


def _prep_w_a2(w_a2):
    z = jnp.zeros((DEPTH, 128, w_a2.shape[-1]), F32)
    return z.at[:, TAIL_LR:TAIL_LR + GLA_LR, :].set(w_a2)


def _stream(x3, mod3, mod_row0, tg, ts, weights, dsa_fn, s0_fn, gla_tt):
    g, s, d = x3.shape
    news = []
    for l in range(DEPTH):
        p = norm_proj(x3, weights["g_mix"], mod3, l * MOD_ROWS + mod_row0, l, weights["w_in"], tg, ts, NP_IN // 3)
        p3 = p.reshape(g, s, NP_IN)
        s0, s0_row0 = s0_fn(l)
        ya, s_new = gla_mixer(p3, weights["w_a2"][l], weights["b_a"][l:l + 1], weights["g_gla"][l:l + 1],
                              s0, s0_row0, gla_tt)
        yb = dsa_fn(p3, l)
        k_b = p3[..., COL_DK:COL_DK + 1024].reshape(g, s, DSA_HEADS, DSA_DH)
        v_b = p3[..., COL_DV:COL_DV + 1024].reshape(g, s, DSA_HEADS, DSA_DH)
        ki = p3[..., COL_TAIL + TAIL_KI:COL_TAIL + TAIL_KI + IDX_DIM]
        news.append((k_b, v_b, ki, s_new))
        x3 = out_proj_residual(x3, mod3, l * MOD_ROWS + mod_row0, l, ya.reshape(g * s, -1), yb.reshape(g * s, -1),
                               weights["w_out"], tg, ts)
        qt, h16 = norm_proj_t(x3, weights["g_ffn"], mod3, l * MOD_ROWS + mod_row0, l, weights["peer_wq_t"], tg, ts)
        s1, s2, e1, e2, tc = peer_route(qt, weights["peer_keys"], l)
        x3 = peer_dense(x3, mod3, l * MOD_ROWS + mod_row0, l, h16, s1, s2, e1, e2, tc,
                        weights["peer_u"], weights["peer_v"], tg, ts)
    y = final_norm(x3, weights["g_final"], tg, ts)
    return y, [jnp.stack([n[i] for n in news]) for i in range(4)]


def kernel(x_prompt, x_sample, c_prompt, c_sample, cache_k, cache_v, cache_kidx, state_gla, page_table, w_ada, b_ada,
           g_mix, g_ffn, w_in, w_a2, b_a, g_gla, w_out, peer_wq, peer_keys, peer_u, peer_v, g_final):
    batch, seq, d = x_prompt.shape
    db, tn, _ = x_sample.shape
    n_pool = cache_k.shape[1]
    n_pages = page_table.shape[1]

    c_all = jnp.concatenate([c_sample, c_prompt, jnp.zeros((MOD_ROWS - db - batch, d), F32)], axis=0)
    mod3 = ada_mod(c_all, w_ada, b_ada).reshape(DEPTH * MOD_ROWS, 1, 6 * d)

    weights = dict(
        g_mix=g_mix.reshape(DEPTH, 1, d), g_ffn=g_ffn.reshape(DEPTH, 1, d),
        w_in=_prep_w_in(w_in), w_a2=_prep_w_a2(w_a2), b_a=b_a, g_gla=g_gla,
        w_out=w_out.astype(BF16), peer_wq_t=jnp.swapaxes(peer_wq, 1, 2).astype(BF16),
        peer_keys=peer_keys, peer_u=peer_u.astype(BF16), peer_v=peer_v.astype(BF16), g_final=g_final,
    )

    def dsa_p(p3, l):
        w = DSA_HEADS * DSA_DH
        q16 = p3[..., COL_DQ:COL_DQ + w].astype(BF16)
        k16 = p3[..., COL_DK:COL_DK + w].astype(BF16)
        v16 = p3[..., COL_DV:COL_DV + w].astype(BF16)
        qi16 = p3[..., COL_DQI:COL_DQI + w].astype(BF16)
        ki16 = p3[..., COL_TAIL + TAIL_KI:COL_TAIL + TAIL_KI + IDX_DIM].astype(BF16)
        return dsa_prompt(q16, qi16, p3, k16, v16, ki16)

    cache_k2 = cache_k.reshape(DEPTH * n_pool, PAGE_SIZE, DSA_HEADS * DSA_DH)
    cache_v2 = cache_v.reshape(DEPTH * n_pool, PAGE_SIZE, DSA_HEADS * DSA_DH)
    cache_kidx2 = cache_kidx.reshape(DEPTH * n_pool, PAGE_SIZE, IDX_DIM)

    def dsa_s(p3, l):
        page_rows = (page_table.astype(jnp.int32) + l * n_pool).reshape(-1)
        return dsa_sample(p3, page_rows, cache_k2, cache_v2, cache_kidx2)

    zero_state = jnp.zeros((batch, GLA_HEADS, GLA_DK, GLA_DV), F32)
    state2 = state_gla.reshape(DEPTH * db, GLA_HEADS, GLA_DK, GLA_DV)
    y_p, sp = _stream(x_prompt, mod3, MOD_PROMPT_ROW0, 1, 512, weights, dsa_p, lambda l: (zero_state, 0), 512)
    y_s, ss = _stream(x_sample, mod3, 0, 64, tn, weights, dsa_s, lambda l: (state2, l * db), tn)
    return (y_p, y_s, sp[0], sp[1], sp[2], sp[3], ss[0], ss[1], ss[2], ss[3])
```

```python
import functools

import jax
import jax.numpy as jnp
from jax import lax
from jax.experimental import pallas as pl
from jax.experimental.pallas import tpu as pltpu

F32 = jnp.float32
BF16 = jnp.bfloat16
HIGHEST = lax.Precision.HIGHEST
NEG_INF = float("-inf")

D_MODEL = 2048
DEPTH = 2
PAGE_SIZE = 128
GLA_HEADS, GLA_DK, GLA_DV, GLA_LR, GLA_TAU, GLA_CHUNK = 4, 128, 256, 16, 16.0, 64
GLA_SUBCHUNK = 16
GLA_EXP_CLAMP = 80.0
DSA_HEADS, DSA_DH, IDX_HEADS, IDX_DIM, DSA_TOPK_MAX = 8, 128, 16, 64, 256
PEER_HEADS, PEER_NKEYS, PEER_DQ, PEER_TOPK = 8, 128, 256, 16
PEER_EXPERTS = PEER_NKEYS * PEER_NKEYS
RMS_EPS = 1e-6
BISECT_ITERS = 32

SPLIT_SIZES = (512, 512, 1024, 1024, 16, 1024, 1024, 1024, 1024, 64, 16)
COL_GQ, COL_GK, COL_GV, COL_GR = 0, 512, 1024, 2048
COL_DQ, COL_DK, COL_DV, COL_DQI, COL_TAIL = 3072, 4096, 5120, 6144, 7168
TAIL_KI, TAIL_LR, TAIL_W = 0, 64, 80
NP_IN = 7296
MOD_ROWS = 192
MOD_PROMPT_ROW0 = 128
(MOD_SH1, MOD_SC1, MOD_GT1, MOD_SH2, MOD_SC2, MOD_GT2) = range(6)

VMEM_LIMIT = 56 * 1024 * 1024


def _cparams(sem):
    return pltpu.CompilerParams(dimension_semantics=sem, vmem_limit_bytes=VMEM_LIMIT)


def _dot_t(a, b):
    return lax.dot_general(a, b, (((1,), (1,)), ((), ())), preferred_element_type=F32)


def _dot_lt(a, b):
    return lax.dot_general(a, b, (((0,), (0,)), ((), ())), preferred_element_type=F32)


def _row_to_col(row):
    return jnp.transpose(jnp.broadcast_to(row, (128, 128)))[:, :1]


def _ada_kernel(c_ref, w_ref, b_ref, o_ref):
    o_ref[0] = jnp.dot(c_ref[...].astype(BF16), w_ref[0].astype(BF16), preferred_element_type=F32) + b_ref[0]


def ada_mod(c_all, w_ada, b_ada, tn=1024):
    m, d = c_all.shape
    n = w_ada.shape[-1]
    return pl.pallas_call(
        _ada_kernel,
        grid=(DEPTH, n // tn),
        in_specs=[
            pl.BlockSpec((m, d), lambda l, j: (0, 0)),
            pl.BlockSpec((1, d, tn), lambda l, j: (l, 0, j)),
            pl.BlockSpec((1, 1, tn), lambda l, j: (l, 0, j)),
        ],
        out_specs=pl.BlockSpec((1, m, tn), lambda l, j: (l, 0, j)),
        out_shape=jax.ShapeDtypeStruct((DEPTH, m, n), F32),
        compiler_params=_cparams(("parallel", "parallel")),
        name="ada_mod",
    )(c_all, w_ada, b_ada.reshape(DEPTH, 1, n))


def _norm_mod(x, g, sc, sh):
    ms = jnp.mean(x * x, axis=-1, keepdims=True)
    y = x * lax.rsqrt(ms + RMS_EPS) * g
    return y * (1.0 + sc) + sh


def _norm_proj_kernel(x_ref, g_ref, sc_ref, sh_ref, w_ref, o_ref):
    h = _norm_mod(x_ref[...], g_ref[...], sc_ref[...], sh_ref[...])
    tg, ts, d = h.shape
    o_ref[...] = jnp.dot(h.reshape(tg * ts, d).astype(BF16), w_ref[0], preferred_element_type=F32)


def _x_specs(g, s, tg, ts, mod_row0, pieces, layer, grid_rank_prefix):
    ns = s // ts
    assert mod_row0 % tg == 0

    def tok(*idx):
        return idx[grid_rank_prefix]

    x_spec = pl.BlockSpec((tg, ts, D_MODEL), lambda *idx: (tok(*idx) // ns, tok(*idx) % ns, 0))
    g_spec = pl.BlockSpec((1, 1, D_MODEL), lambda *idx: (layer, 0, 0))
    mod_specs = [
        pl.BlockSpec((tg, 1, D_MODEL), lambda *idx, p=p: (mod_row0 // tg + tok(*idx) // ns, 0, p))
        for p in pieces
    ]
    return x_spec, g_spec, mod_specs, (g // tg) * ns


def norm_proj(x3, gain3, mod3, mod_row0, layer, w, tg, ts, tn):
    g, s, d = x3.shape
    n = w.shape[-1]
    x_spec, g_spec, mod_specs, mt = _x_specs(g, s, tg, ts, mod_row0, (MOD_SC1, MOD_SH1), layer, 1)
    return pl.pallas_call(
        _norm_proj_kernel,
        grid=(n // tn, mt),
        in_specs=[x_spec, g_spec, *mod_specs, pl.BlockSpec((1, d, tn), lambda j, i: (layer, 0, j))],
        out_specs=pl.BlockSpec((tg * ts, tn), lambda j, i: (i, j)),
        out_shape=jax.ShapeDtypeStruct((g * s, n), F32),
        compiler_params=_cparams(("parallel", "parallel")),
        name="norm_proj",
    )(x3, gain3, mod3, mod3, w)


def _norm_proj_t_kernel(x_ref, g_ref, sc_ref, sh_ref, wt_ref, qt_ref, h_ref):
    h = _norm_mod(x_ref[...], g_ref[...], sc_ref[...], sh_ref[...])
    tg, ts, d = h.shape
    hb = h.reshape(tg * ts, d).astype(BF16)
    h_ref[...] = hb
    qt_ref[...] = _dot_t(wt_ref[0], hb)


def norm_proj_t(x3, gain3, mod3, mod_row0, layer, wt, tg, ts):
    g, s, d = x3.shape
    n = wt.shape[1]
    x_spec, g_spec, mod_specs, mt = _x_specs(g, s, tg, ts, mod_row0, (MOD_SC2, MOD_SH2), layer, 0)
    tm = tg * ts
    return pl.pallas_call(
        _norm_proj_t_kernel,
        grid=(mt,),
        in_specs=[x_spec, g_spec, *mod_specs, pl.BlockSpec((1, n, d), lambda i: (layer, 0, 0))],
        out_specs=[pl.BlockSpec((n, tm), lambda i: (0, i)), pl.BlockSpec((tm, d), lambda i: (i, 0))],
        out_shape=[jax.ShapeDtypeStruct((n, g * s), F32), jax.ShapeDtypeStruct((g * s, d), BF16)],
        compiler_params=_cparams(("parallel",)),
        name="norm_proj_t",
    )(x3, gain3, mod3, mod3, wt)


def _log_sigmoid(z):
    return jnp.minimum(z, 0.0) - jnp.log(1.0 + jnp.exp(-jnp.abs(z)))


def _gla_kernel(q_ref, k_ref, v_ref, r_ref, tail_ref, wa_ref, ba_ref, gg_ref, s0_ref, y_ref, sout_ref, s_ref,
                *, chunk, sub, n_chunks):
    t = pl.program_id(1)

    @pl.when(t == 0)
    def _():
        s_ref[...] = s0_ref[0]

    row = lax.broadcasted_iota(jnp.int32, (chunk, chunk), 0)
    col = lax.broadcasted_iota(jnp.int32, (chunk, chunk), 1)
    tri = row >= col
    tri_f = tri.astype(F32)
    n_sub = chunk // sub

    def do_chunk(c):
        rows = pl.ds(pl.multiple_of(c * chunk, chunk), chunk)
        z = jnp.dot(tail_ref[0, rows, :], wa_ref[...], precision=HIGHEST, preferred_element_type=F32) + ba_ref[...]
        log_a = _log_sigmoid(z) * (1.0 / GLA_TAU)
        bcum = jnp.dot(tri_f, log_a, precision=HIGHEST, preferred_element_type=F32)
        for h in range(GLA_HEADS):
            ksl = slice(h * GLA_DK, (h + 1) * GLA_DK)
            vsl = slice(h * GLA_DV, (h + 1) * GLA_DV)
            q = q_ref[0, rows, ksl] * (GLA_DK ** -0.5)
            k = k_ref[0, rows, ksl]
            v = v_ref[0, rows, vsl]
            b = bcum[:, ksl]
            state = s_ref[h]
            inter = jnp.dot(q * jnp.exp(b), state, preferred_element_type=F32)
            att_rows = []
            for s in range(n_sub):
                ref_b = b[s * sub - 1:s * sub, :] if s > 0 else jnp.zeros((1, GLA_DK), F32)
                qs = q[s * sub:(s + 1) * sub, :] * jnp.exp(b[s * sub:(s + 1) * sub, :] - ref_b)
                ks = k * jnp.exp(jnp.minimum(ref_b - b, GLA_EXP_CLAMP))
                att_rows.append(_dot_t(qs, ks))
            att = att_rows[0] if n_sub == 1 else jnp.concatenate(att_rows, axis=0)
            att = jnp.where(tri, att, 0.0)
            o = inter + jnp.dot(att, v, preferred_element_type=F32)
            b_last = b[chunk - 1:chunk, :]
            decay_col = _row_to_col(jnp.exp(b_last))
            s_ref[h] = state * decay_col + _dot_lt(k * jnp.exp(b_last - b), v)
            o = o * lax.rsqrt(jnp.mean(o * o, axis=-1, keepdims=True) + RMS_EPS)
            y_ref[0, rows, vsl] = o * gg_ref[:, vsl] * jax.nn.silu(r_ref[0, rows, vsl])

    if n_chunks == 1:
        do_chunk(0)
    else:
        def body(c, carry):
            do_chunk(c)
            return carry
        lax.fori_loop(0, n_chunks, body, 0)

    @pl.when(t == pl.num_programs(1) - 1)
    def _():
        sout_ref[0] = s_ref[...]


def gla_mixer(p3, wa_pad, ba, gg, s0, s0_row0, tt):
    b, t, _ = p3.shape
    chunk = min(GLA_CHUNK, t)
    assert t % tt == 0 and tt % chunk == 0
    sub = min(GLA_SUBCHUNK, chunk)
    kern = functools.partial(_gla_kernel, chunk=chunk, sub=sub, n_chunks=tt // chunk)
    kw = GLA_HEADS * GLA_DK
    vw = GLA_HEADS * GLA_DV
    return pl.pallas_call(
        kern,
        grid=(b, t // tt),
        in_specs=[
            pl.BlockSpec((1, tt, kw), lambda i, j: (i, j, COL_GQ // kw)),
            pl.BlockSpec((1, tt, kw), lambda i, j: (i, j, COL_GK // kw)),
            pl.BlockSpec((1, tt, vw), lambda i, j: (i, j, COL_GV // vw)),
            pl.BlockSpec((1, tt, vw), lambda i, j: (i, j, COL_GR // vw)),
            pl.BlockSpec((1, tt, 128), lambda i, j: (i, j, COL_TAIL // 128)),
            pl.BlockSpec((128, kw), lambda i, j: (0, 0)),
            pl.BlockSpec((1, kw), lambda i, j: (0, 0)),
            pl.BlockSpec((1, vw), lambda i, j: (0, 0)),
            pl.BlockSpec((1, GLA_HEADS, GLA_DK, GLA_DV), lambda i, j: (s0_row0 + i, 0, 0, 0)),
        ],
        out_specs=[
            pl.BlockSpec((1, tt, vw), lambda i, j: (i, j, 0)),
            pl.BlockSpec((1, GLA_HEADS, GLA_DK, GLA_DV), lambda i, j: (i, 0, 0, 0)),
        ],
        out_shape=[jax.ShapeDtypeStruct((b, t, vw), F32),
                   jax.ShapeDtypeStruct((b, GLA_HEADS, GLA_DK, GLA_DV), F32)],
        scratch_shapes=[pltpu.VMEM((GLA_HEADS, GLA_DK, GLA_DV), F32)],
        compiler_params=_cparams(("parallel", "arbitrary")),
        name="gla_mixer",
    )(p3, p3, p3, p3, p3, wa_pad, ba, gg, s0)


def _bisect_threshold(read_scores, lo, hi, k, axis, settled):
    def cond(c):
        it, _, _, _, pending = c
        return jnp.logical_and(it < BISECT_ITERS, pending > 0.0)

    def body(c):
        it, lo, hi, todo, _ = c
        mid = 0.5 * lo + 0.5 * hi
        cnt = jnp.sum(jnp.where(read_scores() >= mid, 1.0, 0.0), axis=axis, keepdims=True)
        ok = cnt >= float(k)
        todo = jnp.where(cnt == float(k), 0.0, todo)
        return it + 1, jnp.where(ok, mid, lo), jnp.where(ok, hi, mid), todo, jnp.max(todo)

    todo0 = jnp.where(settled, 0.0, 1.0)
    _, lo, _, _, _ = lax.while_loop(cond, body, (jnp.int32(0), lo, hi, todo0, jnp.max(todo0)))
    return lo


def _dsa_prompt_kernel(q_ref, qi_ref, tail_ref, k_ref, v_ref, ki_ref, o_ref, sc_ref, *, tq, tk, ksel):
    t_keys = k_ref.shape[1]
    qpos = pl.program_id(1) * tq + lax.broadcasted_iota(jnp.int32, (tq, 1), 0)
    kpos = lax.broadcasted_iota(jnp.int32, (1, t_keys), 1)
    causal = kpos <= qpos
    w = tail_ref[0, :, TAIL_W:TAIL_W + IDX_HEADS] * ((IDX_HEADS ** -0.5) * (IDX_DIM ** -0.5))

    for kb in range(t_keys // tk):
        ki = ki_ref[0, kb * tk:(kb + 1) * tk, :]
        acc = jnp.zeros((tq, tk), F32)
        for h in range(IDX_HEADS):
            sc = _dot_t(qi_ref[0, :, h * IDX_DIM:(h + 1) * IDX_DIM], ki)
            acc = acc + w[:, h:h + 1] * jnp.maximum(sc, 0.0)
        sc_ref[:, kb * tk:(kb + 1) * tk] = acc
    score = jnp.where(causal, sc_ref[...], NEG_INF)
    sc_ref[...] = score
    hi = jnp.max(score, axis=1, keepdims=True)
    lo = jnp.min(jnp.where(causal, score, jnp.inf), axis=1, keepdims=True)
    few = qpos + 1 <= ksel
    lo = _bisect_threshold(lambda: sc_ref[...], lo, hi, ksel, 1, few)
    thr = jnp.where(few, NEG_INF, lo)
    sc_ref[...] = jnp.where((sc_ref[...] >= thr) & causal, 0.0, NEG_INF)

    for h in range(DSA_HEADS):
        sl = slice(h * DSA_DH, (h + 1) * DSA_DH)
        s = _dot_t(q_ref[0, :, sl], k_ref[0, :, sl]) * (DSA_DH ** -0.5) + sc_ref[...]
        p = jnp.exp(s - jnp.max(s, axis=1, keepdims=True))
        denom = jnp.sum(p, axis=1, keepdims=True)
        o_ref[0, :, sl] = jnp.dot(p.astype(BF16), v_ref[0, :, sl], preferred_element_type=F32) / denom


def dsa_prompt(q16, qi16, p3, k16, v16, ki16, tq=256, tk=512):
    b, t, _ = q16.shape
    ksel = min(DSA_TOPK_MAX, t // 4)
    w = DSA_HEADS * DSA_DH
    kern = functools.partial(_dsa_prompt_kernel, tq=tq, tk=min(tk, t), ksel=ksel)
    return pl.pallas_call(
        kern,
        grid=(b, t // tq),
        in_specs=[
            pl.BlockSpec((1, tq, w), lambda i, j: (i, j, 0)),
            pl.BlockSpec((1, tq, IDX_HEADS * IDX_DIM), lambda i, j: (i, j, 0)),
            pl.BlockSpec((1, tq, 128), lambda i, j: (i, j, COL_TAIL // 128)),
            pl.BlockSpec((1, t, w), lambda i, j: (i, 0, 0)),
            pl.BlockSpec((1, t, w), lambda i, j: (i, 0, 0)),
            pl.BlockSpec((1, t, IDX_DIM), lambda i, j: (i, 0, 0)),
        ],
        out_specs=pl.BlockSpec((1, tq, w), lambda i, j: (i, j, 0)),
        out_shape=jax.ShapeDtypeStruct((b, t, w), F32),
        scratch_shapes=[pltpu.VMEM((tq, t), F32)],
        compiler_params=_cparams(("parallel", "parallel")),
        name="dsa_prompt",
    )(q16, qi16, p3, k16, v16, ki16)


def _group_reduce_lanes(x, group, op):
    shift = x.shape[-1] // 2
    while shift >= group:
        x = op(x, pltpu.roll(x, shift, x.ndim - 1))
        shift //= 2
    return x


def _multiway_threshold(count_ge, lo, hi, k, ways, iters):
    def body(_, c):
        lo, hi = c
        step = (hi - lo) * (1.0 / ways)
        new_lo, new_hi = lo, hi
        for j in range(1, ways):
            mid = lo + step * float(j)
            ok = count_ge(mid) >= float(k)
            new_lo = jnp.where(ok, jnp.maximum(new_lo, mid), new_lo)
            new_hi = jnp.where(ok, new_hi, jnp.minimum(new_hi, mid))
        return new_lo, new_hi
    lo, hi = lax.fori_loop(0, iters, body, (lo, hi))
    return lo


def _dsa_sample_kernel(pt_ref, q_ref, qi_ref, tail_ref, kn_ref, vn_ref, *refs, n_pages, ksel):
    del pt_ref
    kidx_refs = refs[:n_pages]
    k_refs = refs[n_pages:2 * n_pages]
    v_refs = refs[2 * n_pages:3 * n_pages]
    o_ref, y_ref, r_ref = refs[3 * n_pages:]
    tn = q_ref.shape[1]
    page = PAGE_SIZE
    past = n_pages * page
    ncol = 128
    n_grp = ncol // tn
    assert IDX_HEADS == n_grp and n_pages <= n_grp and DSA_HEADS <= n_grp
    lane_grp = lax.broadcasted_iota(jnp.int32, (1, ncol), 1) // tn

    tail = tail_ref[0]
    w = tail[:, TAIL_W:TAIL_W + IDX_HEADS] * ((IDX_HEADS ** -0.5) * (IDX_DIM ** -0.5))
    qi = qi_ref[0]
    qs = jnp.concatenate([qi[:, h * IDX_DIM:(h + 1) * IDX_DIM] for h in range(IDX_HEADS)], axis=0).astype(BF16)
    wcol = jnp.concatenate([w[:, h:h + 1] for h in range(IDX_HEADS)], axis=0)
    wmat = jnp.transpose(jnp.broadcast_to(wcol, (ncol, ncol)))

    def idx_scores(ki):
        sc = _dot_t(ki, qs)
        y = _group_reduce_lanes(wmat[:ki.shape[0]] * jnp.maximum(sc, 0.0), tn, jnp.add)
        return _group_reduce_lanes(jnp.where(lane_grp == 0, y, 0.0), tn, jnp.add)

    packed = jnp.full((page, ncol), NEG_INF, F32)
    for p in range(n_pages):
        y = idx_scores(kidx_refs[p][0, 0].astype(BF16))
        y_ref[p * page:(p + 1) * page, :] = y
        packed = jnp.where(lane_grp == p, y, packed)
    lane_t = lax.broadcasted_iota(jnp.int32, (tn, ncol), 1) % tn
    new_row = lax.broadcasted_iota(jnp.int32, (tn, ncol), 0)
    y_new = jnp.where(new_row <= lane_t, idx_scores(tail[:, TAIL_KI:TAIL_KI + IDX_DIM].astype(BF16)), NEG_INF)
    packed_new = jnp.where(lane_grp == 0, y_new, NEG_INF)

    def over_keys(fn_reduce, op, fill):
        a = fn_reduce(jnp.where(packed == NEG_INF, fill, packed), axis=0, keepdims=True)
        b = fn_reduce(jnp.where(packed_new == NEG_INF, fill, packed_new), axis=0, keepdims=True)
        return _group_reduce_lanes(op(a, b), tn, op)

    hi = over_keys(jnp.max, jnp.maximum, NEG_INF)
    lo = over_keys(jnp.min, jnp.minimum, jnp.inf)

    def count_ge(mid):
        c = jnp.sum(jnp.where(packed >= mid, 1.0, 0.0), axis=0, keepdims=True)
        c = c + jnp.sum(jnp.where(packed_new >= mid, 1.0, 0.0), axis=0, keepdims=True)
        return _group_reduce_lanes(c, tn, jnp.add)

    thr = _multiway_threshold(count_ge, lo, hi, ksel, 16, BISECT_ITERS // 4)

    q = q_ref[0]
    qst = jnp.concatenate([q[:, h * DSA_DH:(h + 1) * DSA_DH] for h in range(DSA_HEADS)]
                          + [jnp.zeros((ncol - DSA_HEADS * tn, DSA_DH), F32)], axis=0).astype(BF16)
    rows_kh = page * DSA_HEADS
    row_head = lax.broadcasted_iota(jnp.int32, (rows_kh, ncol), 0) % DSA_HEADS
    match = row_head == lax.broadcasted_iota(jnp.int32, (rows_kh, ncol), 1) // tn
    scale = DSA_DH ** -0.5
    m = jnp.full((1, ncol), NEG_INF, F32)
    for p in range(n_pages):
        bias = jnp.where(y_ref[p * page:(p + 1) * page, :] >= thr, 0.0, NEG_INF)
        bias_kh = jnp.broadcast_to(bias[:, None, :], (page, DSA_HEADS, ncol)).reshape(rows_kh, ncol)
        k2 = k_refs[p][0, 0].reshape(rows_kh, DSA_DH).astype(BF16)
        r = jnp.where(match, _dot_t(k2, qst) * scale + bias_kh, NEG_INF)
        r_ref[p * rows_kh:(p + 1) * rows_kh, :] = r
        m = jnp.maximum(m, jnp.max(r, axis=0, keepdims=True))
    bias_new = jnp.where(y_new >= thr, 0.0, NEG_INF)
    kn = kn_ref[0]
    s_new = jnp.full((tn, ncol), NEG_INF, F32)
    for h in range(DSA_HEADS):
        sh = _dot_t(kn[:, h * DSA_DH:(h + 1) * DSA_DH].astype(BF16), qst) * scale + bias_new
        s_new = jnp.where(lane_grp == h, sh, s_new)
    m = jnp.maximum(m, jnp.max(s_new, axis=0, keepdims=True))
    m = jnp.where(m == NEG_INF, 0.0, m)

    denom = jnp.zeros((1, ncol), F32)
    acc = jnp.zeros((ncol, DSA_DH), F32)
    for p in range(n_pages):
        pr = jnp.exp(r_ref[p * rows_kh:(p + 1) * rows_kh, :] - m)
        denom = denom + jnp.sum(pr, axis=0, keepdims=True)
        acc = acc + _dot_lt(pr.astype(BF16), v_refs[p][0, 0].reshape(rows_kh, DSA_DH).astype(BF16))
    p_new = jnp.exp(s_new - m)
    denom = denom + jnp.sum(p_new, axis=0, keepdims=True)
    vn = vn_ref[0]
    row_grp = lax.broadcasted_iota(jnp.int32, (ncol, DSA_DH), 0) // tn
    for h in range(DSA_HEADS):
        acc = acc + jnp.where(row_grp == h, _dot_lt(p_new, vn[:, h * DSA_DH:(h + 1) * DSA_DH]), 0.0)
    denom = jnp.where(denom == 0.0, 1.0, denom)
    acc = acc * _row_to_col(1.0 / denom)
    for h in range(DSA_HEADS):
        o_ref[0, :, h * DSA_DH:(h + 1) * DSA_DH] = acc[h * tn:(h + 1) * tn, :]


def dsa_sample(p3, page_ids, layer, cache_k, cache_v, cache_kidx):
    db, tn, _ = p3.shape
    n_pages = page_ids.shape[0] // db
    length = n_pages * PAGE_SIZE + tn
    ksel = min(DSA_TOPK_MAX, length // 4)
    w = DSA_HEADS * DSA_DH

    def kv_spec(p):
        return pl.BlockSpec((1, 1, PAGE_SIZE, DSA_HEADS, DSA_DH),
                            lambda i, pt, p=p: (layer, pt[i * n_pages + p], 0, 0, 0))

    def kidx_spec(p):
        return pl.BlockSpec((1, 1, PAGE_SIZE, IDX_DIM), lambda i, pt, p=p: (layer, pt[i * n_pages + p], 0, 0))

    in_specs = [
        pl.BlockSpec((1, tn, w), lambda i, pt: (i, 0, COL_DQ // w)),
        pl.BlockSpec((1, tn, w), lambda i, pt: (i, 0, COL_DQI // w)),
        pl.BlockSpec((1, tn, 128), lambda i, pt: (i, 0, COL_TAIL // 128)),
        pl.BlockSpec((1, tn, w), lambda i, pt: (i, 0, COL_DK // w)),
        pl.BlockSpec((1, tn, w), lambda i, pt: (i, 0, COL_DV // w)),
    ]
    in_specs += [kidx_spec(p) for p in range(n_pages)]
    in_specs += [kv_spec(p) for p in range(n_pages)]
    in_specs += [kv_spec(p) for p in range(n_pages)]
    kern = functools.partial(_dsa_sample_kernel, n_pages=n_pages, ksel=ksel)
    return pl.pallas_call(
        kern,
        grid_spec=pltpu.PrefetchScalarGridSpec(
            num_scalar_prefetch=1,
            grid=(db,),
            in_specs=in_specs,
            out_specs=pl.BlockSpec((1, tn, w), lambda i, pt: (i, 0, 0)),
            scratch_shapes=[pltpu.VMEM((length, 128), F32),
                            pltpu.VMEM((n_pages * PAGE_SIZE * DSA_HEADS, 128), F32)],
        ),
        out_shape=jax.ShapeDtypeStruct((db, tn, w), F32),
        compiler_params=_cparams(("parallel",)),
        name="dsa_sample",
    )(page_ids, p3, p3, p3, p3, p3, *([cache_kidx] * n_pages), *([cache_k] * n_pages), *([cache_v] * n_pages))


def _out_proj_kernel(x_ref, gt_ref, ya_ref, yb_ref, w_ref, o_ref):
    wa = ya_ref.shape[-1]
    m = jnp.dot(ya_ref[...].astype(BF16), w_ref[0, :wa, :], preferred_element_type=F32)
    m = m + jnp.dot(yb_ref[...].astype(BF16), w_ref[0, wa:, :], preferred_element_type=F32)
    x = x_ref[...]
    o_ref[...] = x + gt_ref[...] * m.reshape(x.shape)


def out_proj_residual(x3, mod3, mod_row0, layer, ya, yb, w_out16, tg, ts):
    g, s, d = x3.shape
    x_spec, _, mod_specs, mt = _x_specs(g, s, tg, ts, mod_row0, (MOD_GT1,), layer, 0)
    tm = tg * ts
    wa, wb = ya.shape[-1], yb.shape[-1]
    return pl.pallas_call(
        _out_proj_kernel,
        grid=(mt,),
        in_specs=[x_spec, *mod_specs,
                  pl.BlockSpec((tm, wa), lambda i: (i, 0)),
                  pl.BlockSpec((tm, wb), lambda i: (i, 0)),
                  pl.BlockSpec((1, wa + wb, d), lambda i: (layer, 0, 0))],
        out_specs=x_spec,
        out_shape=jax.ShapeDtypeStruct(x3.shape, F32),
        compiler_params=_cparams(("parallel",)),
        name="out_proj",
    )(x3, mod3, ya, yb, w_out16)


def _extract_top(vals_list, n):
    cur = list(vals_list)
    out = []
    for _ in range(n):
        m = cur[0].max(axis=0, keepdims=True)
        for c in cur[1:]:
            m = jnp.maximum(m, c.max(axis=0, keepdims=True))
        out.append(m)
        cur = [jnp.where(c >= m, NEG_INF, c) for c in cur]
    return out


PEER_STAIR = tuple(PEER_TOPK // (a + 1) for a in range(PEER_TOPK))


def _peer_route_kernel(qt_ref, keys_ref, th_ref, e1_ref, s2_ref, e2_ref):
    half = PEER_DQ // 2
    tops = []
    raw = []
    for p in range(2):
        s = jnp.dot(keys_ref[0, p], qt_ref[p * half:(p + 1) * half, :], precision=HIGHEST,
                    preferred_element_type=F32)
        raw.append(s)
        tops.append(_extract_top([s], PEER_TOPK))
    v1, v2 = tops
    v2_all = jnp.concatenate(v2, axis=0)
    cand = [v1[a] + v2_all[:PEER_STAIR[a], :] for a in range(PEER_TOPK)]
    ctop = _extract_top(cand, PEER_TOPK)
    cmax, cut_c = ctop[0], ctop[PEER_TOPK - 1]
    z = jnp.zeros_like(cmax)
    for c in ctop:
        z = z + jnp.exp(c - cmax)
    theta = jnp.full(raw[0].shape, jnp.inf, F32)
    for a in range(PEER_TOPK):
        n_sel = jnp.sum(jnp.where(cand[a] >= cut_c, 1.0, 0.0), axis=0, keepdims=True)
        cut = jnp.full_like(cut_c, jnp.inf)
        for b in range(PEER_STAIR[a]):
            cut = jnp.where(n_sel == float(b + 1), v2[b], cut)
        theta = jnp.where(raw[0] == v1[a], cut, theta)
    th_ref[...] = theta
    e1_ref[...] = jnp.where(raw[0] >= v1[PEER_TOPK - 1], jnp.exp(raw[0] - v1[0]), 0.0)
    s2_ref[...] = raw[1]
    e2_ref[...] = jnp.where(raw[1] >= v2[PEER_TOPK - 1], jnp.exp(raw[1] - v2[0]), 0.0) / z


def peer_route(qt, keys, layer, tm=512):
    n = qt.shape[1]
    rows = PEER_HEADS * PEER_NKEYS
    blk = pl.BlockSpec((PEER_NKEYS, tm), lambda i, h: (h, i))
    shp = jax.ShapeDtypeStruct((rows, n), F32)
    return pl.pallas_call(
        _peer_route_kernel,
        grid=(n // tm, PEER_HEADS),
        in_specs=[pl.BlockSpec((PEER_DQ, tm), lambda i, h: (h, i)),
                  pl.BlockSpec((1, 2, PEER_NKEYS, PEER_DQ // 2), lambda i, h: (layer, 0, 0, 0))],
        out_specs=[blk, blk, blk, blk],
        out_shape=[shp, shp, shp, shp],
        compiler_params=_cparams(("parallel", "parallel")),
        name="peer_route",
    )(qt, keys)


PEER_SUB = 256


def _peer_dense_kernel(x_ref, gt_ref, h_ref, th_ref, e1_ref, s2_ref, e2_ref, u_ref, v_ref, o_ref, acc_ref, *, eb):
    j = pl.program_id(1)

    @pl.when(j == 0)
    def _():
        acc_ref[...] = jnp.zeros_like(acc_ref)

    assert eb // PEER_NKEYS == 8 and PEER_SUB % PEER_NKEYS == 0
    h_tok = h_ref[...]
    th_rows, e1_rows = [], []
    for h in range(PEER_HEADS):
        rows = pl.ds(pl.multiple_of(h * PEER_NKEYS + j * 8, 8), 8)
        th_rows.append(th_ref[rows, :])
        e1_rows.append(e1_ref[rows, :])
    total = None
    for c in range(eb // PEER_SUB):
        esl = slice(c * PEER_SUB, (c + 1) * PEER_SUB)
        act = jax.nn.gelu(_dot_t(u_ref[0, esl, :], h_tok))
        coef_rows = []
        for rr in range(PEER_SUB // PEER_NKEYS):
            r = c * (PEER_SUB // PEER_NKEYS) + rr
            g = jnp.zeros((PEER_NKEYS, act.shape[1]), F32)
            for h in range(PEER_HEADS):
                hs = slice(h * PEER_NKEYS, (h + 1) * PEER_NKEYS)
                g = g + jnp.where(s2_ref[hs, :] >= th_rows[h][r:r + 1, :], e2_ref[hs, :], 0.0) * e1_rows[h][r:r + 1, :]
            coef_rows.append(g * act[rr * PEER_NKEYS:(rr + 1) * PEER_NKEYS, :])
        coef_t = jnp.concatenate(coef_rows, axis=0)
        part = jnp.dot(jnp.transpose(coef_t).astype(BF16), v_ref[0, esl, :], preferred_element_type=F32)
        total = part if total is None else total + part
    acc_ref[...] += total

    @pl.when(j == pl.num_programs(1) - 1)
    def _():
        x = x_ref[...]
        o_ref[...] = x + gt_ref[...] * acc_ref[...].reshape(x.shape)


def peer_dense(x3, mod3, mod_row0, layer, h16, th, e1, s2, e2, u16, v16, tg, ts, eb=1024):
    g, s, d = x3.shape
    x_spec, _, mod_specs, mt = _x_specs(g, s, tg, ts, mod_row0, (MOD_GT2,), layer, 0)
    tm = tg * ts
    rows = PEER_HEADS * PEER_NKEYS
    once = pl.Buffered(1)
    tok = pl.BlockSpec((rows, tm), lambda i, j: (0, i), pipeline_mode=once)
    kern = functools.partial(_peer_dense_kernel, eb=eb)
    return pl.pallas_call(
        kern,
        grid=(mt, PEER_EXPERTS // eb),
        in_specs=[x_spec, *mod_specs,
                  pl.BlockSpec((tm, d), lambda i, j: (i, 0), pipeline_mode=once),
                  tok, tok, tok, tok,
                  pl.BlockSpec((1, eb, d), lambda i, j: (layer, j, 0)),
                  pl.BlockSpec((1, eb, d), lambda i, j: (layer, j, 0))],
        out_specs=x_spec,
        out_shape=jax.ShapeDtypeStruct(x3.shape, F32),
        scratch_shapes=[pltpu.VMEM((tm, d), F32)],
        compiler_params=_cparams(("parallel", "arbitrary")),
        name="peer_dense",
    )(x3, mod3, h16, th, e1, s2, e2, u16, v16)


def _final_norm_kernel(x_ref, g_ref, o_ref):
    x = x_ref[...]
    o_ref[...] = x * lax.rsqrt(jnp.mean(x * x, axis=-1, keepdims=True) + RMS_EPS) * g_ref[...]


def final_norm(x3, g_final, tg, ts):
    g, s, d = x3.shape
    ns = s // ts
    spec = pl.BlockSpec((tg, ts, d), lambda i: (i // ns, i % ns, 0))
    return pl.pallas_call(
        _final_norm_kernel,
        grid=((g // tg) * ns,),
        in_specs=[spec, pl.BlockSpec((1, 1, d), lambda i: (0, 0, 0))],
        out_specs=spec,
        out_shape=jax.ShapeDtypeStruct(x3.shape, F32),
        compiler_params=_cparams(("parallel",)),
        name="final_norm",
    )(x3, g_final.reshape(1, 1, d))


def _prep_w_in(w_in):
    o = [0]
    for sz in SPLIT_SIZES:
        o.append(o[-1] + sz)
    pad = jnp.zeros(w_in.shape[:-1] + (NP_IN - o[-1],), w_in.dtype)
    parts = [w_in[..., o[0]:o[4]], w_in[..., o[5]:o[9]], w_in[..., o[9]:o[10]], w_in[..., o[4]:o[5]],
             w_in[..., o[10]:o[11]], pad]
    return jnp.concatenate(parts, axis=-1).astype(BF16)


def _prep_w_a2(w_a2):
    z = jnp.zeros((DEPTH, 128, w_a2.shape[-1]), F32)
    return z.at[:, TAIL_LR:TAIL_LR + GLA_LR, :].set(w_a2)


def _stream(x3, mod3, mod_row0, tg, ts, weights, dsa_fn, s0_fn, gla_tt):
    g, s, d = x3.shape
    news = []
    for l in range(DEPTH):
        p = norm_proj(x3, weights["g_mix"], mod3, l * MOD_ROWS + mod_row0, l, weights["w_in"], tg, ts, NP_IN // 3)
        p3 = p.reshape(g, s, NP_IN)
        s0, s0_row0 = s0_fn(l)
        ya, s_new = gla_mixer(p3, weights["w_a2"][l], weights["b_a"][l:l + 1], weights["g_gla"][l:l + 1],
                              s0, s0_row0, gla_tt)
        yb = dsa_fn(p3, l)
        k_b = p3[..., COL_DK:COL_DK + 1024].reshape(g, s, DSA_HEADS, DSA_DH)
        v_b = p3[..., COL_DV:COL_DV + 1024].reshape(g, s, DSA_HEADS, DSA_DH)
        ki = p3[..., COL_TAIL + TAIL_KI:COL_TAIL + TAIL_KI + IDX_DIM]
        news.append((k_b, v_b, ki, s_new))
        x3 = out_proj_residual(x3, mod3, l * MOD_ROWS + mod_row0, l, ya.reshape(g * s, -1), yb.reshape(g * s, -1),
                               weights["w_out"], tg, ts)
        qt, h16 = norm_proj_t(x3, weights["g_ffn"], mod3, l * MOD_ROWS + mod_row0, l, weights["peer_wq_t"], tg, ts)
        th, e1, s2, e2 = peer_route(qt, weights["peer_keys"], l)
        x3 = peer_dense(x3, mod3, l * MOD_ROWS + mod_row0, l, h16, th, e1, s2, e2,
                        weights["peer_u"], weights["peer_v"], tg, ts)
    y = final_norm(x3, weights["g_final"], tg, ts)
    return y, [jnp.stack([n[i] for n in news]) for i in range(4)]


def kernel(x_prompt, x_sample, c_prompt, c_sample, cache_k, cache_v, cache_kidx, state_gla, page_table, w_ada, b_ada,
           g_mix, g_ffn, w_in, w_a2, b_a, g_gla, w_out, peer_wq, peer_keys, peer_u, peer_v, g_final):
    batch, seq, d = x_prompt.shape
    db, tn, _ = x_sample.shape
    c_all = jnp.concatenate([c_sample, c_prompt, jnp.zeros((MOD_ROWS - db - batch, d), F32)], axis=0)
    mod3 = ada_mod(c_all, w_ada, b_ada).reshape(DEPTH * MOD_ROWS, 1, 6 * d)

    weights = dict(
        g_mix=g_mix.reshape(DEPTH, 1, d), g_ffn=g_ffn.reshape(DEPTH, 1, d),
        w_in=_prep_w_in(w_in), w_a2=_prep_w_a2(w_a2), b_a=b_a, g_gla=g_gla,
        w_out=w_out.astype(BF16), peer_wq_t=jnp.swapaxes(peer_wq, 1, 2).astype(BF16),
        peer_keys=peer_keys, peer_u=peer_u.astype(BF16), peer_v=peer_v.astype(BF16), g_final=g_final,
    )

    def dsa_p(p3, l):
        w = DSA_HEADS * DSA_DH
        q16 = p3[..., COL_DQ:COL_DQ + w].astype(BF16)
        k16 = p3[..., COL_DK:COL_DK + w].astype(BF16)
        v16 = p3[..., COL_DV:COL_DV + w].astype(BF16)
        qi16 = p3[..., COL_DQI:COL_DQI + w].astype(BF16)
        ki16 = p3[..., COL_TAIL + TAIL_KI:COL_TAIL + TAIL_KI + IDX_DIM].astype(BF16)
        return dsa_prompt(q16, qi16, p3, k16, v16, ki16)

    page_ids = page_table.astype(jnp.int32).reshape(-1)

    def dsa_s(p3, l):
        return dsa_sample(p3, page_ids, l, cache_k, cache_v, cache_kidx)

    zero_state = jnp.zeros((batch, GLA_HEADS, GLA_DK, GLA_DV), F32)
    state2 = state_gla.reshape(DEPTH * db, GLA_HEADS, GLA_DK, GLA_DV)
    y_p, sp = _stream(x_prompt, mod3, MOD_PROMPT_ROW0, 1, 512, weights, dsa_p, lambda l: (zero_state, 0), 512)
    y_s, ss = _stream(x_sample, mod3, 0, 64, tn, weights, dsa_s, lambda l: (state2, l * db), tn)
    return (y_p, y_s, sp[0], sp[1], sp[2], sp[3], ss[0], ss[1], ss[2], ss[3])
```

```python
import functools

import jax
import jax.numpy as jnp
from jax import lax
from jax.experimental import pallas as pl
from jax.experimental.pallas import tpu as pltpu

F32 = jnp.float32
BF16 = jnp.bfloat16
HIGHEST = lax.Precision.HIGHEST
NEG_INF = float("-inf")

D_MODEL = 2048
DEPTH = 2
PAGE_SIZE = 128
GLA_HEADS, GLA_DK, GLA_DV, GLA_LR, GLA_TAU, GLA_CHUNK = 4, 128, 256, 16, 16.0, 64
GLA_SUBCHUNK = 16
GLA_EXP_CLAMP = 80.0
DSA_HEADS, DSA_DH, IDX_HEADS, IDX_DIM, DSA_TOPK_MAX = 8, 128, 16, 64, 256
PEER_HEADS, PEER_NKEYS, PEER_DQ, PEER_TOPK = 8, 128, 256, 16
PEER_EXPERTS = PEER_NKEYS * PEER_NKEYS
RMS_EPS = 1e-6
BISECT_ITERS = 32

SPLIT_SIZES = (512, 512, 1024, 1024, 16, 1024, 1024, 1024, 1024, 64, 16)
COL_GQ, COL_GK, COL_GV, COL_GR = 0, 512, 1024, 2048
COL_DQ, COL_DK, COL_DV, COL_DQI, COL_TAIL = 3072, 4096, 5120, 6144, 7168
TAIL_KI, TAIL_LR, TAIL_W = 0, 64, 80
NP_IN = 7296
MOD_ROWS = 192
MOD_PROMPT_ROW0 = 128
(MOD_SH1, MOD_SC1, MOD_GT1, MOD_SH2, MOD_SC2, MOD_GT2) = range(6)

VMEM_LIMIT = 56 * 1024 * 1024


def _cparams(sem):
    return pltpu.CompilerParams(dimension_semantics=sem, vmem_limit_bytes=VMEM_LIMIT)


def _dot_t(a, b):
    return lax.dot_general(a, b, (((1,), (1,)), ((), ())), preferred_element_type=F32)


def _dot_lt(a, b):
    return lax.dot_general(a, b, (((0,), (0,)), ((), ())), preferred_element_type=F32)


def _row_to_col(row):
    return jnp.transpose(jnp.broadcast_to(row, (128, 128)))[:, :1]


def _ada_kernel(c_ref, w_ref, b_ref, o_ref):
    o_ref[0] = jnp.dot(c_ref[...].astype(BF16), w_ref[0].astype(BF16), preferred_element_type=F32) + b_ref[0]


def ada_mod(c_all, w_ada, b_ada, tn=1024):
    m, d = c_all.shape
    n = w_ada.shape[-1]
    return pl.pallas_call(
        _ada_kernel,
        grid=(DEPTH, n // tn),
        in_specs=[
            pl.BlockSpec((m, d), lambda l, j: (0, 0)),
            pl.BlockSpec((1, d, tn), lambda l, j: (l, 0, j)),
            pl.BlockSpec((1, 1, tn), lambda l, j: (l, 0, j)),
        ],
        out_specs=pl.BlockSpec((1, m, tn), lambda l, j: (l, 0, j)),
        out_shape=jax.ShapeDtypeStruct((DEPTH, m, n), F32),
        compiler_params=_cparams(("parallel", "parallel")),
        name="ada_mod",
    )(c_all, w_ada, b_ada.reshape(DEPTH, 1, n))


def _norm_mod(x, g, sc, sh):
    ms = jnp.mean(x * x, axis=-1, keepdims=True)
    y = x * lax.rsqrt(ms + RMS_EPS) * g
    return y * (1.0 + sc) + sh


def _norm_proj_kernel(x_ref, g_ref, sc_ref, sh_ref, w_ref, o_ref):
    h = _norm_mod(x_ref[...], g_ref[...], sc_ref[...], sh_ref[...])
    tg, ts, d = h.shape
    o_ref[...] = jnp.dot(h.reshape(tg * ts, d).astype(BF16), w_ref[0], preferred_element_type=F32)


def _x_specs(g, s, tg, ts, mod_row0, pieces, layer, grid_rank_prefix):
    ns = s // ts
    assert mod_row0 % tg == 0

    def tok(*idx):
        return idx[grid_rank_prefix]

    x_spec = pl.BlockSpec((tg, ts, D_MODEL), lambda *idx: (tok(*idx) // ns, tok(*idx) % ns, 0))
    g_spec = pl.BlockSpec((1, 1, D_MODEL), lambda *idx: (layer, 0, 0))
    mod_specs = [
        pl.BlockSpec((tg, 1, D_MODEL), lambda *idx, p=p: (mod_row0 // tg + tok(*idx) // ns, 0, p))
        for p in pieces
    ]
    return x_spec, g_spec, mod_specs, (g // tg) * ns


def norm_proj(x3, gain3, mod3, mod_row0, layer, w, tg, ts, tn):
    g, s, d = x3.shape
    n = w.shape[-1]
    x_spec, g_spec, mod_specs, mt = _x_specs(g, s, tg, ts, mod_row0, (MOD_SC1, MOD_SH1), layer, 1)
    return pl.pallas_call(
        _norm_proj_kernel,
        grid=(n // tn, mt),
        in_specs=[x_spec, g_spec, *mod_specs, pl.BlockSpec((1, d, tn), lambda j, i: (layer, 0, j))],
        out_specs=pl.BlockSpec((tg * ts, tn), lambda j, i: (i, j)),
        out_shape=jax.ShapeDtypeStruct((g * s, n), F32),
        compiler_params=_cparams(("parallel", "parallel")),
        name="norm_proj",
    )(x3, gain3, mod3, mod3, w)


def _norm_proj_t_kernel(x_ref, g_ref, sc_ref, sh_ref, wt_ref, qt_ref, h_ref):
    h = _norm_mod(x_ref[...], g_ref[...], sc_ref[...], sh_ref[...])
    tg, ts, d = h.shape
    hb = h.reshape(tg * ts, d).astype(BF16)
    h_ref[...] = hb
    qt_ref[...] = _dot_t(wt_ref[0], hb)


def norm_proj_t(x3, gain3, mod3, mod_row0, layer, wt, tg, ts):
    g, s, d = x3.shape
    n = wt.shape[1]
    x_spec, g_spec, mod_specs, mt = _x_specs(g, s, tg, ts, mod_row0, (MOD_SC2, MOD_SH2), layer, 0)
    tm = tg * ts
    return pl.pallas_call(
        _norm_proj_t_kernel,
        grid=(mt,),
        in_specs=[x_spec, g_spec, *mod_specs, pl.BlockSpec((1, n, d), lambda i: (layer, 0, 0))],
        out_specs=[pl.BlockSpec((n, tm), lambda i: (0, i)), pl.BlockSpec((tm, d), lambda i: (i, 0))],
        out_shape=[jax.ShapeDtypeStruct((n, g * s), F32), jax.ShapeDtypeStruct((g * s, d), BF16)],
        compiler_params=_cparams(("parallel",)),
        name="norm_proj_t",
    )(x3, gain3, mod3, mod3, wt)


def _log_sigmoid(z):
    return jnp.minimum(z, 0.0) - jnp.log(1.0 + jnp.exp(-jnp.abs(z)))


def _gla_kernel(q_ref, k_ref, v_ref, r_ref, tail_ref, wa_ref, ba_ref, gg_ref, s0_ref, y_ref, sout_ref, s_ref,
                *, chunk, sub, n_chunks):
    t = pl.program_id(1)

    @pl.when(t == 0)
    def _():
        s_ref[...] = s0_ref[0]

    row = lax.broadcasted_iota(jnp.int32, (chunk, chunk), 0)
    col = lax.broadcasted_iota(jnp.int32, (chunk, chunk), 1)
    tri = row >= col
    tri_f = tri.astype(F32)
    n_sub = chunk // sub

    def do_chunk(c):
        rows = pl.ds(pl.multiple_of(c * chunk, chunk), chunk)
        z = jnp.dot(tail_ref[0, rows, :], wa_ref[...], precision=HIGHEST, preferred_element_type=F32) + ba_ref[...]
        log_a = _log_sigmoid(z) * (1.0 / GLA_TAU)
        bcum = jnp.dot(tri_f, log_a, precision=HIGHEST, preferred_element_type=F32)
        for h in range(GLA_HEADS):
            ksl = slice(h * GLA_DK, (h + 1) * GLA_DK)
            vsl = slice(h * GLA_DV, (h + 1) * GLA_DV)
            q = q_ref[0, rows, ksl] * (GLA_DK ** -0.5)
            k = k_ref[0, rows, ksl]
            v = v_ref[0, rows, vsl]
            b = bcum[:, ksl]
            state = s_ref[h]
            inter = jnp.dot(q * jnp.exp(b), state, preferred_element_type=F32)
            att_rows = []
            for s in range(n_sub):
                ref_b = b[s * sub - 1:s * sub, :] if s > 0 else jnp.zeros((1, GLA_DK), F32)
                qs = q[s * sub:(s + 1) * sub, :] * jnp.exp(b[s * sub:(s + 1) * sub, :] - ref_b)
                ks = k * jnp.exp(jnp.minimum(ref_b - b, GLA_EXP_CLAMP))
                att_rows.append(_dot_t(qs, ks))
            att = att_rows[0] if n_sub == 1 else jnp.concatenate(att_rows, axis=0)
            att = jnp.where(tri, att, 0.0)
            o = inter + jnp.dot(att, v, preferred_element_type=F32)
            b_last = b[chunk - 1:chunk, :]
            decay_col = _row_to_col(jnp.exp(b_last))
            s_ref[h] = state * decay_col + _dot_lt(k * jnp.exp(b_last - b), v)
            o = o * lax.rsqrt(jnp.mean(o * o, axis=-1, keepdims=True) + RMS_EPS)
            y_ref[0, rows, vsl] = o * gg_ref[:, vsl] * jax.nn.silu(r_ref[0, rows, vsl])

    if n_chunks == 1:
        do_chunk(0)
    else:
        def body(c, carry):
            do_chunk(c)
            return carry
        lax.fori_loop(0, n_chunks, body, 0)

    @pl.when(t == pl.num_programs(1) - 1)
    def _():
        sout_ref[0] = s_ref[...]


def gla_mixer(p3, wa_pad, ba, gg, s0, s0_row0, tt):
    b, t, _ = p3.shape
    chunk = min(GLA_CHUNK, t)
    assert t % tt == 0 and tt % chunk == 0
    sub = min(GLA_SUBCHUNK, chunk)
    kern = functools.partial(_gla_kernel, chunk=chunk, sub=sub, n_chunks=tt // chunk)
    kw = GLA_HEADS * GLA_DK
    vw = GLA_HEADS * GLA_DV
    return pl.pallas_call(
        kern,
        grid=(b, t // tt),
        in_specs=[
            pl.BlockSpec((1, tt, kw), lambda i, j: (i, j, COL_GQ // kw)),
            pl.BlockSpec((1, tt, kw), lambda i, j: (i, j, COL_GK // kw)),
            pl.BlockSpec((1, tt, vw), lambda i, j: (i, j, COL_GV // vw)),
            pl.BlockSpec((1, tt, vw), lambda i, j: (i, j, COL_GR // vw)),
            pl.BlockSpec((1, tt, 128), lambda i, j: (i, j, COL_TAIL // 128)),
            pl.BlockSpec((128, kw), lambda i, j: (0, 0)),
            pl.BlockSpec((1, kw), lambda i, j: (0, 0)),
            pl.BlockSpec((1, vw), lambda i, j: (0, 0)),
            pl.BlockSpec((1, GLA_HEADS, GLA_DK, GLA_DV), lambda i, j: (s0_row0 + i, 0, 0, 0)),
        ],
        out_specs=[
            pl.BlockSpec((1, tt, vw), lambda i, j: (i, j, 0)),
            pl.BlockSpec((1, GLA_HEADS, GLA_DK, GLA_DV), lambda i, j: (i, 0, 0, 0)),
        ],
        out_shape=[jax.ShapeDtypeStruct((b, t, vw), F32),
                   jax.ShapeDtypeStruct((b, GLA_HEADS, GLA_DK, GLA_DV), F32)],
        scratch_shapes=[pltpu.VMEM((GLA_HEADS, GLA_DK, GLA_DV), F32)],
        compiler_params=_cparams(("parallel", "arbitrary")),
        name="gla_mixer",
    )(p3, p3, p3, p3, p3, wa_pad, ba, gg, s0)


def _bisect_threshold(read_scores, lo, hi, k, axis, settled):
    def cond(c):
        it, _, _, _, pending = c
        return jnp.logical_and(it < BISECT_ITERS, pending > 0.0)

    def body(c):
        it, lo, hi, todo, _ = c
        mid = 0.5 * lo + 0.5 * hi
        cnt = jnp.sum(jnp.where(read_scores() >= mid, 1.0, 0.0), axis=axis, keepdims=True)
        ok = cnt >= float(k)
        todo = jnp.where(cnt == float(k), 0.0, todo)
        return it + 1, jnp.where(ok, mid, lo), jnp.where(ok, hi, mid), todo, jnp.max(todo)

    todo0 = jnp.where(settled, 0.0, 1.0)
    _, lo, _, _, _ = lax.while_loop(cond, body, (jnp.int32(0), lo, hi, todo0, jnp.max(todo0)))
    return lo


def _dsa_prompt_kernel(q_ref, qi_ref, tail_ref, k_ref, v_ref, ki_ref, o_ref, sc_ref, *, tq, tk, ksel):
    t_keys = k_ref.shape[1]
    qpos = pl.program_id(1) * tq + lax.broadcasted_iota(jnp.int32, (tq, 1), 0)
    kpos = lax.broadcasted_iota(jnp.int32, (1, t_keys), 1)
    causal = kpos <= qpos
    w = tail_ref[0, :, TAIL_W:TAIL_W + IDX_HEADS] * ((IDX_HEADS ** -0.5) * (IDX_DIM ** -0.5))

    for kb in range(t_keys // tk):
        ki = ki_ref[0, kb * tk:(kb + 1) * tk, :]
        acc = jnp.zeros((tq, tk), F32)
        for h in range(IDX_HEADS):
            sc = _dot_t(qi_ref[0, :, h * IDX_DIM:(h + 1) * IDX_DIM], ki)
            acc = acc + w[:, h:h + 1] * jnp.maximum(sc, 0.0)
        sc_ref[:, kb * tk:(kb + 1) * tk] = acc
    score = jnp.where(causal, sc_ref[...], NEG_INF)
    sc_ref[...] = score
    hi = jnp.max(score, axis=1, keepdims=True)
    lo = jnp.min(jnp.where(causal, score, jnp.inf), axis=1, keepdims=True)
    few = qpos + 1 <= ksel
    lo = _bisect_threshold(lambda: sc_ref[...], lo, hi, ksel, 1, few)
    thr = jnp.where(few, NEG_INF, lo)
    sc_ref[...] = jnp.where((sc_ref[...] >= thr) & causal, 0.0, NEG_INF)

    for h in range(DSA_HEADS):
        sl = slice(h * DSA_DH, (h + 1) * DSA_DH)
        s = _dot_t(q_ref[0, :, sl], k_ref[0, :, sl]) * (DSA_DH ** -0.5) + sc_ref[...]
        p = jnp.exp(s - jnp.max(s, axis=1, keepdims=True))
        denom = jnp.sum(p, axis=1, keepdims=True)
        o_ref[0, :, sl] = jnp.dot(p.astype(BF16), v_ref[0, :, sl], preferred_element_type=F32) / denom


def dsa_prompt(q16, qi16, p3, k16, v16, ki16, tq=256, tk=512):
    b, t, _ = q16.shape
    ksel = min(DSA_TOPK_MAX, t // 4)
    w = DSA_HEADS * DSA_DH
    kern = functools.partial(_dsa_prompt_kernel, tq=tq, tk=min(tk, t), ksel=ksel)
    return pl.pallas_call(
        kern,
        grid=(b, t // tq),
        in_specs=[
            pl.BlockSpec((1, tq, w), lambda i, j: (i, j, 0)),
            pl.BlockSpec((1, tq, IDX_HEADS * IDX_DIM), lambda i, j: (i, j, 0)),
            pl.BlockSpec((1, tq, 128), lambda i, j: (i, j, COL_TAIL // 128)),
            pl.BlockSpec((1, t, w), lambda i, j: (i, 0, 0)),
            pl.BlockSpec((1, t, w), lambda i, j: (i, 0, 0)),
            pl.BlockSpec((1, t, IDX_DIM), lambda i, j: (i, 0, 0)),
        ],
        out_specs=pl.BlockSpec((1, tq, w), lambda i, j: (i, j, 0)),
        out_shape=jax.ShapeDtypeStruct((b, t, w), F32),
        scratch_shapes=[pltpu.VMEM((tq, t), F32)],
        compiler_params=_cparams(("parallel", "parallel")),
        name="dsa_prompt",
    )(q16, qi16, p3, k16, v16, ki16)


def _multiway_threshold(count_ge, lo, hi, k, ways, iters):
    def body(_, c):
        lo, hi = c
        step = (hi - lo) * (1.0 / ways)
        new_lo, new_hi = lo, hi
        for j in range(1, ways):
            mid = lo + step * float(j)
            ok = count_ge(mid) >= float(k)
            new_lo = jnp.where(ok, jnp.maximum(new_lo, mid), new_lo)
            new_hi = jnp.where(ok, new_hi, jnp.minimum(new_hi, mid))
        return new_lo, new_hi
    lo, hi = lax.fori_loop(0, iters, body, (lo, hi))
    return lo


def _dsa_sample_kernel(pt_ref, q_ref, qi_ref, tail_ref, kn_ref, vn_ref, *refs, n_pages, ksel):
    del pt_ref
    kidx_refs = refs[:n_pages]
    k_refs = refs[n_pages:2 * n_pages]
    v_refs = refs[2 * n_pages:3 * n_pages]
    o_ref, r_ref = refs[3 * n_pages:]
    tn = q_ref.shape[1]
    page = PAGE_SIZE
    rows_q = DSA_HEADS * tn
    lanes_kh = page * DSA_HEADS
    assert tn == 8 and page == 128

    tail = tail_ref[0]
    w = tail[:, TAIL_W:TAIL_W + IDX_HEADS] * ((IDX_HEADS ** -0.5) * (IDX_DIM ** -0.5))
    qi = qi_ref[0]
    qs = jnp.concatenate([qi[:, h * IDX_DIM:(h + 1) * IDX_DIM] for h in range(IDX_HEADS)], axis=0).astype(BF16)
    wcol = jnp.broadcast_to(jnp.concatenate([w[:, h:h + 1] for h in range(IDX_HEADS)], axis=0), (IDX_HEADS * tn, page))

    def idx_scores(ki):
        x = wcol * jnp.maximum(_dot_t(qs, ki), 0.0)
        return x.reshape(IDX_HEADS, tn, page).sum(axis=0)

    blocks = [idx_scores(kidx_refs[p][0, 0].astype(BF16)) for p in range(n_pages)]
    ki_new = jnp.concatenate([tail[:, TAIL_KI:TAIL_KI + IDX_DIM], jnp.zeros((page - tn, IDX_DIM), F32)], axis=0)
    lane = lax.broadcasted_iota(jnp.int32, (tn, page), 1)
    row_t = lax.broadcasted_iota(jnp.int32, (tn, page), 0)
    s_new = jnp.where(lane <= row_t, idx_scores(ki_new.astype(BF16)), NEG_INF)
    s_all = jnp.concatenate(blocks + [s_new], axis=1)
    hi = jnp.max(s_all, axis=1, keepdims=True)
    lo = jnp.min(jnp.where(s_all == NEG_INF, jnp.inf, s_all), axis=1, keepdims=True)

    def count_ge(mid):
        return jnp.sum(jnp.where(s_all >= mid, 1.0, 0.0), axis=1, keepdims=True)

    thr = _multiway_threshold(count_ge, lo, hi, ksel, 16, BISECT_ITERS // 4)


    q = q_ref[0]
    qst = jnp.concatenate([q[:, h * DSA_DH:(h + 1) * DSA_DH] for h in range(DSA_HEADS)], axis=0).astype(BF16)
    match = (lax.broadcasted_iota(jnp.int32, (rows_q, lanes_kh), 1) % DSA_HEADS
             == lax.broadcasted_iota(jnp.int32, (rows_q, lanes_kh), 0) // tn)
    expand = (lax.broadcasted_iota(jnp.int32, (page, lanes_kh), 1) // DSA_HEADS
              == lax.broadcasted_iota(jnp.int32, (page, lanes_kh), 0)).astype(BF16)
    scale = DSA_DH ** -0.5
    m_acc = jnp.full((rows_q, lanes_kh), NEG_INF, F32)
    for p in range(n_pages):
        sel = jnp.where(blocks[p] >= thr, 1.0, 0.0).astype(BF16)
        sel_kh = jnp.dot(sel, expand, preferred_element_type=F32)
        bias = jnp.where(sel_kh > 0.5, 0.0, NEG_INF)
        bias = jnp.where(match, jnp.concatenate([bias] * DSA_HEADS, axis=0), NEG_INF)
        k2 = k_refs[p][0, 0].reshape(lanes_kh, DSA_DH).astype(BF16)
        r = _dot_t(qst, k2) * scale + bias
        r_ref[:, p * lanes_kh:(p + 1) * lanes_kh] = r
        m_acc = jnp.maximum(m_acc, r)
    pad_rows = jnp.zeros((page - tn, DSA_HEADS * DSA_DH), F32)
    kn = jnp.concatenate([kn_ref[0], pad_rows], axis=0).astype(BF16)
    bias_new = jnp.where(s_new >= thr, 0.0, NEG_INF)
    r_new = jnp.concatenate(
        [_dot_t(qst[h * tn:(h + 1) * tn, :], kn[:, h * DSA_DH:(h + 1) * DSA_DH]) * scale + bias_new
         for h in range(DSA_HEADS)], axis=0)
    m = jnp.maximum(jnp.max(m_acc, axis=1, keepdims=True), jnp.max(r_new, axis=1, keepdims=True))

    l_acc = jnp.zeros((rows_q, lanes_kh), F32)
    acc = jnp.zeros((rows_q, DSA_DH), F32)
    for p in range(n_pages):
        pr = jnp.exp(r_ref[:, p * lanes_kh:(p + 1) * lanes_kh] - m)
        l_acc = l_acc + pr
        v2 = v_refs[p][0, 0].reshape(lanes_kh, DSA_DH).astype(BF16)
        acc = acc + jnp.dot(pr.astype(BF16), v2, preferred_element_type=F32)
    p_new = jnp.exp(r_new - m)
    denom = jnp.sum(l_acc, axis=1, keepdims=True) + jnp.sum(p_new, axis=1, keepdims=True)
    vn = jnp.concatenate([vn_ref[0], pad_rows], axis=0).astype(BF16)
    acc = acc + jnp.concatenate(
        [jnp.dot(p_new[h * tn:(h + 1) * tn, :].astype(BF16), vn[:, h * DSA_DH:(h + 1) * DSA_DH],
                 preferred_element_type=F32) for h in range(DSA_HEADS)], axis=0)
    out = acc / denom
    for h in range(DSA_HEADS):
        o_ref[0, :, h * DSA_DH:(h + 1) * DSA_DH] = out[h * tn:(h + 1) * tn, :]


def dsa_sample(p3, page_ids, layer, cache_k, cache_v, cache_kidx):
    db, tn, _ = p3.shape
    n_pages = page_ids.shape[0] // db
    length = n_pages * PAGE_SIZE + tn
    ksel = min(DSA_TOPK_MAX, length // 4)
    w = DSA_HEADS * DSA_DH

    def kv_spec(p):
        return pl.BlockSpec((1, 1, PAGE_SIZE, DSA_HEADS, DSA_DH),
                            lambda i, pt, p=p: (layer, pt[i * n_pages + p], 0, 0, 0))

    def kidx_spec(p):
        return pl.BlockSpec((1, 1, PAGE_SIZE, IDX_DIM), lambda i, pt, p=p: (layer, pt[i * n_pages + p], 0, 0))

    in_specs = [
        pl.BlockSpec((1, tn, w), lambda i, pt: (i, 0, COL_DQ // w)),
        pl.BlockSpec((1, tn, w), lambda i, pt: (i, 0, COL_DQI // w)),
        pl.BlockSpec((1, tn, 128), lambda i, pt: (i, 0, COL_TAIL // 128)),
        pl.BlockSpec((1, tn, w), lambda i, pt: (i, 0, COL_DK // w)),
        pl.BlockSpec((1, tn, w), lambda i, pt: (i, 0, COL_DV // w)),
    ]
    in_specs += [kidx_spec(p) for p in range(n_pages)]
    in_specs += [kv_spec(p) for p in range(n_pages)]
    in_specs += [kv_spec(p) for p in range(n_pages)]
    kern = functools.partial(_dsa_sample_kernel, n_pages=n_pages, ksel=ksel)
    return pl.pallas_call(
        kern,
        grid_spec=pltpu.PrefetchScalarGridSpec(
            num_scalar_prefetch=1,
            grid=(db,),
            in_specs=in_specs,
            out_specs=pl.BlockSpec((1, tn, w), lambda i, pt: (i, 0, 0)),
            scratch_shapes=[pltpu.VMEM((DSA_HEADS * tn, n_pages * PAGE_SIZE * DSA_HEADS), F32)],
        ),
        out_shape=jax.ShapeDtypeStruct((db, tn, w), F32),
        compiler_params=_cparams(("parallel",)),
        name="dsa_sample",
    )(page_ids, p3, p3, p3, p3, p3, *([cache_kidx] * n_pages), *([cache_k] * n_pages), *([cache_v] * n_pages))


def _out_proj_kernel(x_ref, gt_ref, ya_ref, yb_ref, w_ref, o_ref):
    wa = ya_ref.shape[-1]
    m = jnp.dot(ya_ref[...].astype(BF16), w_ref[0, :wa, :], preferred_element_type=F32)
    m = m + jnp.dot(yb_ref[...].astype(BF16), w_ref[0, wa:, :], preferred_element_type=F32)
    x = x_ref[...]
    o_ref[...] = x + gt_ref[...] * m.reshape(x.shape)


def out_proj_residual(x3, mod3, mod_row0, layer, ya, yb, w_out16, tg, ts):
    g, s, d = x3.shape
    x_spec, _, mod_specs, mt = _x_specs(g, s, tg, ts, mod_row0, (MOD_GT1,), layer, 0)
    tm = tg * ts
    wa, wb = ya.shape[-1], yb.shape[-1]
    return pl.pallas_call(
        _out_proj_kernel,
        grid=(mt,),
        in_specs=[x_spec, *mod_specs,
                  pl.BlockSpec((tm, wa), lambda i: (i, 0)),
                  pl.BlockSpec((tm, wb), lambda i: (i, 0)),
                  pl.BlockSpec((1, wa + wb, d), lambda i: (layer, 0, 0))],
        out_specs=x_spec,
        out_shape=jax.ShapeDtypeStruct(x3.shape, F32),
        compiler_params=_cparams(("parallel",)),
        name="out_proj",
    )(x3, mod3, ya, yb, w_out16)


def _extract_top(vals_list, n):
    cur = list(vals_list)
    out = []
    for _ in range(n):
        m = cur[0].max(axis=0, keepdims=True)
        for c in cur[1:]:
            m = jnp.maximum(m, c.max(axis=0, keepdims=True))
        out.append(m)
        cur = [jnp.where(c >= m, NEG_INF, c) for c in cur]
    return out


PEER_STAIR = tuple(PEER_TOPK // (a + 1) for a in range(PEER_TOPK))


def _extract_top_ranked(s, n):
    cur = s
    rank = jnp.full(s.shape, float(n), F32)
    out = []
    for r in range(n):
        m = cur.max(axis=0, keepdims=True)
        out.append(m)
        hit = cur >= m
        rank = jnp.where(hit, float(r), rank)
        cur = jnp.where(hit, NEG_INF, cur)
    return out, rank


def _peer_route_kernel(qt_ref, keys_ref, m1_ref, e1_ref, r2_ref, e2_ref):
    half = PEER_DQ // 2
    raw = [jnp.dot(keys_ref[0, p], qt_ref[p * half:(p + 1) * half, :], precision=HIGHEST,
                   preferred_element_type=F32) for p in range(2)]
    v1, rank1 = _extract_top_ranked(raw[0], PEER_TOPK)
    v2, rank2 = _extract_top_ranked(raw[1], PEER_TOPK)
    v2_all = jnp.concatenate(v2, axis=0)
    cand = [v1[a] + v2_all[:PEER_STAIR[a], :] for a in range(PEER_TOPK)]
    ctop = _extract_top(cand, PEER_TOPK)
    cmax, cut_c = ctop[0], ctop[PEER_TOPK - 1]
    z = jnp.zeros_like(cmax)
    for c in ctop:
        z = z + jnp.exp(c - cmax)
    m1 = jnp.zeros(raw[0].shape, F32)
    for a in range(PEER_TOPK):
        n_sel = jnp.sum(jnp.where(cand[a] >= cut_c, 1.0, 0.0), axis=0, keepdims=True)
        m1 = jnp.where(rank1 == float(a), n_sel, m1)
    m1_ref[...] = m1
    e1_ref[...] = jnp.where(rank1 < float(PEER_TOPK), jnp.exp(raw[0] - v1[0]), 0.0)
    r2_ref[...] = rank2.astype(r2_ref.dtype)
    e2_ref[...] = (jnp.where(rank2 < float(PEER_TOPK), jnp.exp(raw[1] - v2[0]), 0.0) / z).astype(e2_ref.dtype)


def peer_route(qt, keys, layer, tm=512):
    n = qt.shape[1]
    rows = PEER_HEADS * PEER_NKEYS
    blk = pl.BlockSpec((PEER_NKEYS, tm), lambda i, h: (h, i))
    shp = jax.ShapeDtypeStruct((rows, n), F32)
    shp16 = jax.ShapeDtypeStruct((rows, n), BF16)
    return pl.pallas_call(
        _peer_route_kernel,
        grid=(n // tm, PEER_HEADS),
        in_specs=[pl.BlockSpec((PEER_DQ, tm), lambda i, h: (h, i)),
                  pl.BlockSpec((1, 2, PEER_NKEYS, PEER_DQ // 2), lambda i, h: (layer, 0, 0, 0))],
        out_specs=[blk, blk, blk, blk],
        out_shape=[shp, shp, shp16, shp16],
        compiler_params=_cparams(("parallel", "parallel")),
        name="peer_route",
    )(qt, keys)


PEER_SUB = 256


def _peer_dense_kernel(x_ref, gt_ref, h_ref, m1_ref, e1_ref, r2_ref, e2_ref, u_ref, v_ref, o_ref, acc_ref, *, eb):
    j = pl.program_id(1)

    @pl.when(j == 0)
    def _():
        acc_ref[...] = jnp.zeros_like(acc_ref)

    assert eb // PEER_NKEYS == 8 and PEER_SUB % PEER_NKEYS == 0
    h_tok = h_ref[...]
    gdt = r2_ref.dtype
    m1_rows, e1_rows = [], []
    for h in range(PEER_HEADS):
        rows = pl.ds(pl.multiple_of(h * PEER_NKEYS + j * 8, 8), 8)
        m1_rows.append(m1_ref[rows, :].astype(gdt))
        e1_rows.append(e1_ref[rows, :].astype(gdt))
    total = None
    for c in range(eb // PEER_SUB):
        esl = slice(c * PEER_SUB, (c + 1) * PEER_SUB)
        act = jax.nn.gelu(_dot_t(u_ref[0, esl, :], h_tok))
        coef_rows = []
        for rr in range(PEER_SUB // PEER_NKEYS):
            r = c * (PEER_SUB // PEER_NKEYS) + rr
            g = jnp.zeros((PEER_NKEYS, act.shape[1]), gdt)
            for h in range(PEER_HEADS):
                hs = slice(h * PEER_NKEYS, (h + 1) * PEER_NKEYS)
                picked = jnp.where(r2_ref[hs, :] < m1_rows[h][r:r + 1, :], e2_ref[hs, :], jnp.zeros((), gdt))
                g = g + picked * e1_rows[h][r:r + 1, :]
            coef_rows.append(g.astype(F32) * act[rr * PEER_NKEYS:(rr + 1) * PEER_NKEYS, :])
        coef_t = jnp.concatenate(coef_rows, axis=0)
        part = jnp.dot(jnp.transpose(coef_t).astype(BF16), v_ref[0, esl, :], preferred_element_type=F32)
        total = part if total is None else total + part
    acc_ref[...] += total

    @pl.when(j == pl.num_programs(1) - 1)
    def _():
        x = x_ref[...]
        o_ref[...] = x + gt_ref[...] * acc_ref[...].reshape(x.shape)


def peer_dense(x3, mod3, mod_row0, layer, h16, m1, e1, r2, e2, u16, v16, tg, ts, eb=1024):
    g, s, d = x3.shape
    x_spec, _, mod_specs, mt = _x_specs(g, s, tg, ts, mod_row0, (MOD_GT2,), layer, 0)
    tm = tg * ts
    rows = PEER_HEADS * PEER_NKEYS
    once = pl.Buffered(1)
    tok = pl.BlockSpec((rows, tm), lambda i, j: (0, i), pipeline_mode=once)
    kern = functools.partial(_peer_dense_kernel, eb=eb)
    return pl.pallas_call(
        kern,
        grid=(mt, PEER_EXPERTS // eb),
        in_specs=[x_spec, *mod_specs,
                  pl.BlockSpec((tm, d), lambda i, j: (i, 0), pipeline_mode=once),
                  tok, tok, tok, tok,
                  pl.BlockSpec((1, eb, d), lambda i, j: (layer, j, 0)),
                  pl.BlockSpec((1, eb, d), lambda i, j: (layer, j, 0))],
        out_specs=x_spec,
        out_shape=jax.ShapeDtypeStruct(x3.shape, F32),
        scratch_shapes=[pltpu.VMEM((tm, d), F32)],
        compiler_params=_cparams(("parallel", "arbitrary")),
        name="peer_dense",
    )(x3, mod3, h16, m1, e1, r2, e2, u16, v16)


def _final_norm_kernel(x_ref, g_ref, o_ref):
    x = x_ref[...]
    o_ref[...] = x * lax.rsqrt(jnp.mean(x * x, axis=-1, keepdims=True) + RMS_EPS) * g_ref[...]


def final_norm(x3, g_final, tg, ts):
    g, s, d = x3.shape
    ns = s // ts
    spec = pl.BlockSpec((tg, ts, d), lambda i: (i // ns, i % ns, 0))
    return pl.pallas_call(
        _final_norm_kernel,
        grid=((g // tg) * ns,),
        in_specs=[spec, pl.BlockSpec((1, 1, d), lambda i: (0, 0, 0))],
        out_specs=spec,
        out_shape=jax.ShapeDtypeStruct(x3.shape, F32),
        compiler_params=_cparams(("parallel",)),
        name="final_norm",
    )(x3, g_final.reshape(1, 1, d))


def _prep_w_in(w_in):
    o = [0]
    for sz in SPLIT_SIZES:
        o.append(o[-1] + sz)
    pad = jnp.zeros(w_in.shape[:-1] + (NP_IN - o[-1],), w_in.dtype)
    parts = [w_in[..., o[0]:o[4]], w_in[..., o[5]:o[9]], w_in[..., o[9]:o[10]], w_in[..., o[4]:o[5]],
             w_in[..., o[10]:o[11]], pad]
    return jnp.concatenate(parts, axis=-1).astype(BF16)


def _prep_w_a2(w_a2):
    z = jnp.zeros((DEPTH, 128, w_a2.shape[-1]), F32)
    return z.at[:, TAIL_LR:TAIL_LR + GLA_LR, :].set(w_a2)


def _stream(x3, mod3, mod_row0, tg, ts, weights, dsa_fn, s0_fn, gla_tt):
    g, s, d = x3.shape
    news = []
    for l in range(DEPTH):
        p = norm_proj(x3, weights["g_mix"], mod3, l * MOD_ROWS + mod_row0, l, weights["w_in"], tg, ts, NP_IN // 3)
        p3 = p.reshape(g, s, NP_IN)
        s0, s0_row0 = s0_fn(l)
        ya, s_new = gla_mixer(p3, weights["w_a2"][l], weights["b_a"][l:l + 1], weights["g_gla"][l:l + 1],
                              s0, s0_row0, gla_tt)
        yb = dsa_fn(p3, l)
        k_b = p3[..., COL_DK:COL_DK + 1024].reshape(g, s, DSA_HEADS, DSA_DH)
        v_b = p3[..., COL_DV:COL_DV + 1024].reshape(g, s, DSA_HEADS, DSA_DH)
        ki = p3[..., COL_TAIL + TAIL_KI:COL_TAIL + TAIL_KI + IDX_DIM]
        news.append((k_b, v_b, ki, s_new))
        x3 = out_proj_residual(x3, mod3, l * MOD_ROWS + mod_row0, l, ya.reshape(g * s, -1), yb.reshape(g * s, -1),
                               weights["w_out"], tg, ts)
        qt, h16 = norm_proj_t(x3, weights["g_ffn"], mod3, l * MOD_ROWS + mod_row0, l, weights["peer_wq_t"], tg, ts)
        m1, e1, r2, e2 = peer_route(qt, weights["peer_keys"], l)
        x3 = peer_dense(x3, mod3, l * MOD_ROWS + mod_row0, l, h16, m1, e1, r2, e2,
                        weights["peer_u"], weights["peer_v"], tg, ts)
    y = final_norm(x3, weights["g_final"], tg, ts)
    return y, [jnp.stack([n[i] for n in news]) for i in range(4)]


def kernel(x_prompt, x_sample, c_prompt, c_sample, cache_k, cache_v, cache_kidx, state_gla, page_table, w_ada, b_ada,
           g_mix, g_ffn, w_in, w_a2, b_a, g_gla, w_out, peer_wq, peer_keys, peer_u, peer_v, g_final):
    batch, seq, d = x_prompt.shape
    db, tn, _ = x_sample.shape
    c_all = jnp.concatenate([c_sample, c_prompt, jnp.zeros((MOD_ROWS - db - batch, d), F32)], axis=0)
    mod3 = ada_mod(c_all, w_ada, b_ada).reshape(DEPTH * MOD_ROWS, 1, 6 * d)

    weights = dict(
        g_mix=g_mix.reshape(DEPTH, 1, d), g_ffn=g_ffn.reshape(DEPTH, 1, d),
        w_in=_prep_w_in(w_in), w_a2=_prep_w_a2(w_a2), b_a=b_a, g_gla=g_gla,
        w_out=w_out.astype(BF16), peer_wq_t=jnp.swapaxes(peer_wq, 1, 2).astype(BF16),
        peer_keys=peer_keys, peer_u=peer_u.astype(BF16), peer_v=peer_v.astype(BF16), g_final=g_final,
    )

    def dsa_p(p3, l):
        w = DSA_HEADS * DSA_DH
        q16 = p3[..., COL_DQ:COL_DQ + w].astype(BF16)
        k16 = p3[..., COL_DK:COL_DK + w].astype(BF16)
        v16 = p3[..., COL_DV:COL_DV + w].astype(BF16)
        qi16 = p3[..., COL_DQI:COL_DQI + w].astype(BF16)
        ki16 = p3[..., COL_TAIL + TAIL_KI:COL_TAIL + TAIL_KI + IDX_DIM].astype(BF16)
        return dsa_prompt(q16, qi16, p3, k16, v16, ki16)

    page_ids = page_table.astype(jnp.int32).reshape(-1)

    def dsa_s(p3, l):
        return dsa_sample(p3, page_ids, l, cache_k, cache_v, cache_kidx)

    zero_state = jnp.zeros((batch, GLA_HEADS, GLA_DK, GLA_DV), F32)
    state2 = state_gla.reshape(DEPTH * db, GLA_HEADS, GLA_DK, GLA_DV)
    y_p, sp = _stream(x_prompt, mod3, MOD_PROMPT_ROW0, 1, 512, weights, dsa_p, lambda l: (zero_state, 0), 512)
    y_s, ss = _stream(x_sample, mod3, 0, 64, tn, weights, dsa_s, lambda l: (state2, l * db), tn)
    return (y_p, y_s, sp[0], sp[1], sp[2], sp[3], ss[0], ss[1], ss[2], ss[3])
```

```python
import functools

import jax
import jax.numpy as jnp
from jax import lax
from jax.experimental import pallas as pl
from jax.experimental.pallas import tpu as pltpu

F32 = jnp.float32
BF16 = jnp.bfloat16
HIGHEST = lax.Precision.HIGHEST
NEG_INF = float("-inf")

D_MODEL = 2048
DEPTH = 2
PAGE_SIZE = 128
GLA_HEADS, GLA_DK, GLA_DV, GLA_LR, GLA_TAU, GLA_CHUNK = 4, 128, 256, 16, 16.0, 64
GLA_SUBCHUNK = 16
GLA_EXP_CLAMP = 80.0
DSA_HEADS, DSA_DH, IDX_HEADS, IDX_DIM, DSA_TOPK_MAX = 8, 128, 16, 64, 256
PEER_HEADS, PEER_NKEYS, PEER_DQ, PEER_TOPK = 8, 128, 256, 16
PEER_EXPERTS = PEER_NKEYS * PEER_NKEYS
RMS_EPS = 1e-6
BISECT_ITERS = 32

SPLIT_SIZES = (512, 512, 1024, 1024, 16, 1024, 1024, 1024, 1024, 64, 16)
COL_GQ, COL_GK, COL_GV, COL_GR = 0, 512, 1024, 2048
COL_DQ, COL_DK, COL_DV, COL_DQI, COL_TAIL = 3072, 4096, 5120, 6144, 7168
TAIL_KI, TAIL_LR, TAIL_W = 0, 64, 80
NP_IN = 7296
MOD_ROWS = 192
MOD_PROMPT_ROW0 = 128
(MOD_SH1, MOD_SC1, MOD_GT1, MOD_SH2, MOD_SC2, MOD_GT2) = range(6)

VMEM_LIMIT = 56 * 1024 * 1024


def _cparams(sem):
    return pltpu.CompilerParams(dimension_semantics=sem, vmem_limit_bytes=VMEM_LIMIT)


def _dot_t(a, b):
    return lax.dot_general(a, b, (((1,), (1,)), ((), ())), preferred_element_type=F32)


def _dot_lt(a, b):
    return lax.dot_general(a, b, (((0,), (0,)), ((), ())), preferred_element_type=F32)


def _row_to_col(row):
    return jnp.transpose(jnp.broadcast_to(row, (128, 128)))[:, :1]


def _ada_kernel(c_ref, w_ref, b_ref, o_ref):
    o_ref[0] = jnp.dot(c_ref[...].astype(BF16), w_ref[0].astype(BF16), preferred_element_type=F32) + b_ref[0]


def ada_mod(c_all, w_ada, b_ada, tn=1024):
    m, d = c_all.shape
    n = w_ada.shape[-1]
    return pl.pallas_call(
        _ada_kernel,
        grid=(DEPTH, n // tn),
        in_specs=[
            pl.BlockSpec((m, d), lambda l, j: (0, 0)),
            pl.BlockSpec((1, d, tn), lambda l, j: (l, 0, j)),
            pl.BlockSpec((1, 1, tn), lambda l, j: (l, 0, j)),
        ],
        out_specs=pl.BlockSpec((1, m, tn), lambda l, j: (l, 0, j)),
        out_shape=jax.ShapeDtypeStruct((DEPTH, m, n), F32),
        compiler_params=_cparams(("parallel", "parallel")),
        name="ada_mod",
    )(c_all, w_ada, b_ada.reshape(DEPTH, 1, n))


def _norm_mod(x, g, sc, sh):
    ms = jnp.mean(x * x, axis=-1, keepdims=True)
    y = x * lax.rsqrt(ms + RMS_EPS) * g
    return y * (1.0 + sc) + sh


def _norm_proj_kernel(x_ref, g_ref, sc_ref, sh_ref, w_ref, o_ref, *maybe_ob_ref):
    h = _norm_mod(x_ref[...], g_ref[...], sc_ref[...], sh_ref[...])
    tg, ts, d = h.shape
    res = jnp.dot(h.reshape(tg * ts, d).astype(BF16), w_ref[0], preferred_element_type=F32)
    o_ref[...] = res
    for ob_ref in maybe_ob_ref:
        ob_ref[...] = res.astype(ob_ref.dtype)


def _x_specs(g, s, tg, ts, mod_row0, pieces, layer, grid_rank_prefix):
    ns = s // ts
    assert mod_row0 % tg == 0

    def tok(*idx):
        return idx[grid_rank_prefix]

    x_spec = pl.BlockSpec((tg, ts, D_MODEL), lambda *idx: (tok(*idx) // ns, tok(*idx) % ns, 0))
    g_spec = pl.BlockSpec((1, 1, D_MODEL), lambda *idx: (layer, 0, 0))
    mod_specs = [
        pl.BlockSpec((tg, 1, D_MODEL), lambda *idx, p=p: (mod_row0 // tg + tok(*idx) // ns, 0, p))
        for p in pieces
    ]
    return x_spec, g_spec, mod_specs, (g // tg) * ns


def norm_proj(x3, gain3, mod3, mod_row0, layer, w, tg, ts, tn, with_bf16_copy):
    g, s, d = x3.shape
    n = w.shape[-1]
    x_spec, g_spec, mod_specs, mt = _x_specs(g, s, tg, ts, mod_row0, (MOD_SC1, MOD_SH1), layer, 1)
    out_spec = pl.BlockSpec((tg * ts, tn), lambda j, i: (i, j))
    out_shape = [jax.ShapeDtypeStruct((g * s, n), F32)]
    if with_bf16_copy:
        out_shape.append(jax.ShapeDtypeStruct((g * s, n), BF16))
    return pl.pallas_call(
        _norm_proj_kernel,
        grid=(n // tn, mt),
        in_specs=[x_spec, g_spec, *mod_specs, pl.BlockSpec((1, d, tn), lambda j, i: (layer, 0, j))],
        out_specs=[out_spec] * len(out_shape),
        out_shape=out_shape,
        compiler_params=_cparams(("parallel", "parallel")),
        name="norm_proj",
    )(x3, gain3, mod3, mod3, w)


def _norm_proj_t_kernel(x_ref, g_ref, sc_ref, sh_ref, wt_ref, qt_ref, h_ref):
    h = _norm_mod(x_ref[...], g_ref[...], sc_ref[...], sh_ref[...])
    tg, ts, d = h.shape
    hb = h.reshape(tg * ts, d).astype(BF16)
    h_ref[...] = hb
    qt_ref[...] = _dot_t(wt_ref[0], hb)


def norm_proj_t(x3, gain3, mod3, mod_row0, layer, wt, tg, ts):
    g, s, d = x3.shape
    n = wt.shape[1]
    x_spec, g_spec, mod_specs, mt = _x_specs(g, s, tg, ts, mod_row0, (MOD_SC2, MOD_SH2), layer, 0)
    tm = tg * ts
    return pl.pallas_call(
        _norm_proj_t_kernel,
        grid=(mt,),
        in_specs=[x_spec, g_spec, *mod_specs, pl.BlockSpec((1, n, d), lambda i: (layer, 0, 0))],
        out_specs=[pl.BlockSpec((n, tm), lambda i: (0, i)), pl.BlockSpec((tm, d), lambda i: (i, 0))],
        out_shape=[jax.ShapeDtypeStruct((n, g * s), F32), jax.ShapeDtypeStruct((g * s, d), BF16)],
        compiler_params=_cparams(("parallel",)),
        name="norm_proj_t",
    )(x3, gain3, mod3, mod3, wt)


def _log_sigmoid(z):
    return jnp.minimum(z, 0.0) - jnp.log(1.0 + jnp.exp(-jnp.abs(z)))


def _gla_kernel(q_ref, k_ref, v_ref, r_ref, tail_ref, wa_ref, ba_ref, gg_ref, s0_ref, y_ref, sout_ref, s_ref,
                *, chunk, sub, n_chunks):
    t = pl.program_id(1)

    @pl.when(t == 0)
    def _():
        s_ref[...] = s0_ref[0]

    row = lax.broadcasted_iota(jnp.int32, (chunk, chunk), 0)
    col = lax.broadcasted_iota(jnp.int32, (chunk, chunk), 1)
    tri = row >= col
    tri_f = tri.astype(F32)
    n_sub = chunk // sub

    def do_chunk(c):
        rows = pl.ds(pl.multiple_of(c * chunk, chunk), chunk)
        z = jnp.dot(tail_ref[0, rows, :], wa_ref[...], precision=HIGHEST, preferred_element_type=F32) + ba_ref[...]
        log_a = _log_sigmoid(z) * (1.0 / GLA_TAU)
        bcum = jnp.dot(tri_f, log_a, precision=HIGHEST, preferred_element_type=F32)
        for h in range(GLA_HEADS):
            ksl = slice(h * GLA_DK, (h + 1) * GLA_DK)
            vsl = slice(h * GLA_DV, (h + 1) * GLA_DV)
            q = q_ref[0, rows, ksl] * (GLA_DK ** -0.5)
            k = k_ref[0, rows, ksl]
            v = v_ref[0, rows, vsl]
            b = bcum[:, ksl]
            state = s_ref[h]
            inter = jnp.dot(q * jnp.exp(b), state, preferred_element_type=F32)
            att_rows = []
            for s in range(n_sub):
                ref_b = b[s * sub - 1:s * sub, :] if s > 0 else jnp.zeros((1, GLA_DK), F32)
                qs = q[s * sub:(s + 1) * sub, :] * jnp.exp(b[s * sub:(s + 1) * sub, :] - ref_b)
                ks = k * jnp.exp(jnp.minimum(ref_b - b, GLA_EXP_CLAMP))
                att_rows.append(_dot_t(qs, ks))
            att = att_rows[0] if n_sub == 1 else jnp.concatenate(att_rows, axis=0)
            att = jnp.where(tri, att, 0.0)
            o = inter + jnp.dot(att, v, preferred_element_type=F32)
            b_last = b[chunk - 1:chunk, :]
            decay_col = _row_to_col(jnp.exp(b_last))
            s_ref[h] = state * decay_col + _dot_lt(k * jnp.exp(b_last - b), v)
            o = o * lax.rsqrt(jnp.mean(o * o, axis=-1, keepdims=True) + RMS_EPS)
            y_ref[0, rows, vsl] = o * gg_ref[:, vsl] * jax.nn.silu(r_ref[0, rows, vsl])

    if n_chunks == 1:
        do_chunk(0)
    else:
        def body(c, carry):
            do_chunk(c)
            return carry
        lax.fori_loop(0, n_chunks, body, 0)

    @pl.when(t == pl.num_programs(1) - 1)
    def _():
        sout_ref[0] = s_ref[...]


def gla_mixer(p3, wa_pad, ba, gg, s0, s0_row0, tt):
    b, t, _ = p3.shape
    chunk = min(GLA_CHUNK, t)
    assert t % tt == 0 and tt % chunk == 0
    sub = min(GLA_SUBCHUNK, chunk)
    kern = functools.partial(_gla_kernel, chunk=chunk, sub=sub, n_chunks=tt // chunk)
    kw = GLA_HEADS * GLA_DK
    vw = GLA_HEADS * GLA_DV
    return pl.pallas_call(
        kern,
        grid=(b, t // tt),
        in_specs=[
            pl.BlockSpec((1, tt, kw), lambda i, j: (i, j, COL_GQ // kw)),
            pl.BlockSpec((1, tt, kw), lambda i, j: (i, j, COL_GK // kw)),
            pl.BlockSpec((1, tt, vw), lambda i, j: (i, j, COL_GV // vw)),
            pl.BlockSpec((1, tt, vw), lambda i, j: (i, j, COL_GR // vw)),
            pl.BlockSpec((1, tt, 128), lambda i, j: (i, j, COL_TAIL // 128)),
            pl.BlockSpec((128, kw), lambda i, j: (0, 0)),
            pl.BlockSpec((1, kw), lambda i, j: (0, 0)),
            pl.BlockSpec((1, vw), lambda i, j: (0, 0)),
            pl.BlockSpec((1, GLA_HEADS, GLA_DK, GLA_DV), lambda i, j: (s0_row0 + i, 0, 0, 0)),
        ],
        out_specs=[
            pl.BlockSpec((1, tt, vw), lambda i, j: (i, j, 0)),
            pl.BlockSpec((1, GLA_HEADS, GLA_DK, GLA_DV), lambda i, j: (i, 0, 0, 0)),
        ],
        out_shape=[jax.ShapeDtypeStruct((b, t, vw), F32),
                   jax.ShapeDtypeStruct((b, GLA_HEADS, GLA_DK, GLA_DV), F32)],
        scratch_shapes=[pltpu.VMEM((GLA_HEADS, GLA_DK, GLA_DV), F32)],
        compiler_params=_cparams(("parallel", "arbitrary")),
        name="gla_mixer",
    )(p3, p3, p3, p3, p3, wa_pad, ba, gg, s0)


def _bisect_threshold(count_ge, lo, hi, k, settled):
    def cond(c):
        it, _, _, _, pending = c
        return jnp.logical_and(it < BISECT_ITERS, pending > 0.0)

    def body(c):
        it, lo, hi, todo, _ = c
        mid = 0.5 * lo + 0.5 * hi
        cnt = count_ge(mid)
        ok = cnt >= float(k)
        todo = jnp.where(cnt == float(k), 0.0, todo)
        return it + 1, jnp.where(ok, mid, lo), jnp.where(ok, hi, mid), todo, jnp.max(todo)

    todo0 = jnp.where(settled, 0.0, 1.0)
    _, lo, _, _, _ = lax.while_loop(cond, body, (jnp.int32(0), lo, hi, todo0, jnp.max(todo0)))
    return lo


def _dsa_prompt_kernel(q_ref, qi_ref, tail_ref, k_ref, v_ref, ki_ref, o_ref, sc_ref, r_ref, wb_ref, *, tq, ksel):
    qt = pl.program_id(1)
    nkb = qt + 1
    qpos = qt * tq + lax.broadcasted_iota(jnp.int32, (tq, 1), 0)
    diag_ok = lax.broadcasted_iota(jnp.int32, (tq, tq), 1) <= lax.broadcasted_iota(jnp.int32, (tq, tq), 0)
    w = tail_ref[0, :, TAIL_W:TAIL_W + IDX_HEADS] * ((IDX_HEADS ** -0.5) * (IDX_DIM ** -0.5))
    for h in range(IDX_HEADS):
        wb_ref[h] = jnp.broadcast_to(w[:, h:h + 1], (tq, tq))

    def keys_of(kb):
        return pl.ds(pl.multiple_of(kb * tq, tq), tq)

    def visible(kb):
        return jnp.logical_or(kb < qt, diag_ok)

    def idx_block(kb, carry):
        hi, lo = carry
        ki = ki_ref[0, keys_of(kb), :IDX_DIM]
        acc = jnp.zeros((tq, tq), F32)
        for h in range(IDX_HEADS):
            sc = _dot_t(qi_ref[0, :, h * IDX_DIM:(h + 1) * IDX_DIM], ki)
            acc = acc + wb_ref[h] * jnp.maximum(sc, 0.0)
        vis = visible(kb)
        sc_ref[kb] = jnp.where(vis, acc, NEG_INF)
        hi = jnp.maximum(hi, jnp.max(jnp.where(vis, acc, NEG_INF), axis=1, keepdims=True))
        lo = jnp.minimum(lo, jnp.min(jnp.where(vis, acc, jnp.inf), axis=1, keepdims=True))
        return hi, lo

    hi, lo = lax.fori_loop(0, nkb, idx_block, (jnp.full((tq, 1), NEG_INF, F32), jnp.full((tq, 1), jnp.inf, F32)))

    def count_ge(mid):
        def body(kb, c):
            hit = jnp.where(sc_ref[kb] >= mid, 1.0, 0.0)
            return c + hit[:, :tq // 2] + hit[:, tq // 2:]
        part = lax.fori_loop(0, nkb, body, jnp.zeros((tq, tq // 2), F32))
        return jnp.sum(part, axis=1, keepdims=True)

    few = qpos + 1 <= ksel
    lo = _bisect_threshold(count_ge, lo, hi, ksel, few)
    thr = jnp.where(few, NEG_INF, lo)

    def bias_block(kb, carry):
        sc_ref[kb] = jnp.where(jnp.logical_and(sc_ref[kb] >= thr, visible(kb)), 0.0, NEG_INF)
        return carry

    lax.fori_loop(0, nkb, bias_block, 0)

    scale = DSA_DH ** -0.5

    def scores(kb, ms):
        out = []
        for h in range(DSA_HEADS):
            sl = slice(h * DSA_DH, (h + 1) * DSA_DH)
            s = _dot_t(q_ref[0, :, sl], k_ref[0, keys_of(kb), sl]) * scale + sc_ref[kb]
            r_ref[h, kb] = s
            out.append(jnp.maximum(ms[h], jnp.max(s, axis=1, keepdims=True)))
        return tuple(out)

    ms = lax.fori_loop(0, nkb, scores, tuple(jnp.full((tq, 1), NEG_INF, F32) for _ in range(DSA_HEADS)))

    def weighted(kb, carry):
        out = []
        for h in range(DSA_HEADS):
            sl = slice(h * DSA_DH, (h + 1) * DSA_DH)
            denom, acc = carry[h]
            p = jnp.exp(r_ref[h, kb] - ms[h])
            out.append((denom + jnp.sum(p, axis=1, keepdims=True),
                        acc + jnp.dot(p.astype(BF16), v_ref[0, keys_of(kb), sl], preferred_element_type=F32)))
        return tuple(out)

    init = tuple((jnp.zeros((tq, 1), F32), jnp.zeros((tq, DSA_DH), F32)) for _ in range(DSA_HEADS))
    res = lax.fori_loop(0, nkb, weighted, init)
    for h in range(DSA_HEADS):
        o_ref[0, :, h * DSA_DH:(h + 1) * DSA_DH] = res[h][1] / res[h][0]


def dsa_prompt(pb3, p3, tq=256):
    b, t, _ = pb3.shape
    ksel = min(DSA_TOPK_MAX, t // 4)
    w = DSA_HEADS * DSA_DH
    kern = functools.partial(_dsa_prompt_kernel, tq=tq, ksel=ksel)
    return pl.pallas_call(
        kern,
        grid=(b, t // tq),
        in_specs=[
            pl.BlockSpec((1, tq, w), lambda i, j: (i, j, COL_DQ // w)),
            pl.BlockSpec((1, tq, w), lambda i, j: (i, j, COL_DQI // w)),
            pl.BlockSpec((1, tq, 128), lambda i, j: (i, j, COL_TAIL // 128)),
            pl.BlockSpec((1, t, w), lambda i, j: (i, 0, COL_DK // w)),
            pl.BlockSpec((1, t, w), lambda i, j: (i, 0, COL_DV // w)),
            pl.BlockSpec((1, t, 128), lambda i, j: (i, 0, COL_TAIL // 128)),
        ],
        out_specs=pl.BlockSpec((1, tq, w), lambda i, j: (i, j, 0)),
        out_shape=jax.ShapeDtypeStruct((b, t, w), F32),
        scratch_shapes=[pltpu.VMEM((t // tq, tq, tq), F32), pltpu.VMEM((DSA_HEADS, t // tq, tq, tq), F32),
                        pltpu.VMEM((IDX_HEADS, tq, tq), F32)],
        compiler_params=_cparams(("parallel", "parallel")),
        name="dsa_prompt",
    )(pb3, pb3, p3, pb3, pb3, pb3)


def _multiway_threshold(count_ge, lo, hi, k, ways, iters):
    def body(_, c):
        lo, hi = c
        step = (hi - lo) * (1.0 / ways)
        new_lo, new_hi = lo, hi
        for j in range(1, ways):
            mid = lo + step * float(j)
            ok = count_ge(mid) >= float(k)
            new_lo = jnp.where(ok, jnp.maximum(new_lo, mid), new_lo)
            new_hi = jnp.where(ok, new_hi, jnp.minimum(new_hi, mid))
        return new_lo, new_hi
    lo, hi = lax.fori_loop(0, iters, body, (lo, hi))
    return lo


def _dsa_sample_kernel(pt_ref, q_ref, qi_ref, tail_ref, kn_ref, vn_ref, *refs, n_pages, ksel):
    del pt_ref
    kidx_refs = refs[:n_pages]
    k_refs = refs[n_pages:2 * n_pages]
    v_refs = refs[2 * n_pages:3 * n_pages]
    o_ref, r_ref = refs[3 * n_pages:]
    tn = q_ref.shape[1]
    page = PAGE_SIZE
    rows_q = DSA_HEADS * tn
    lanes_kh = page * DSA_HEADS
    assert tn == 8 and page == 128

    tail = tail_ref[0]
    w = tail[:, TAIL_W:TAIL_W + IDX_HEADS] * ((IDX_HEADS ** -0.5) * (IDX_DIM ** -0.5))
    qi = qi_ref[0]
    qs = jnp.concatenate([qi[:, h * IDX_DIM:(h + 1) * IDX_DIM] for h in range(IDX_HEADS)], axis=0).astype(BF16)
    wcol = jnp.broadcast_to(jnp.concatenate([w[:, h:h + 1] for h in range(IDX_HEADS)], axis=0), (IDX_HEADS * tn, page))

    def idx_scores(raw):
        x = wcol * jnp.maximum(raw, 0.0)
        return x.reshape(IDX_HEADS, tn, page).sum(axis=0)

    blocks = [idx_scores(jnp.dot(qs, kidx_refs[p][0, 0].astype(BF16), preferred_element_type=F32))
              for p in range(n_pages)]
    ki_new = jnp.concatenate([tail[:, TAIL_KI:TAIL_KI + IDX_DIM], jnp.zeros((page - tn, IDX_DIM), F32)], axis=0)
    lane = lax.broadcasted_iota(jnp.int32, (tn, page), 1)
    row_t = lax.broadcasted_iota(jnp.int32, (tn, page), 0)
    s_new = jnp.where(lane <= row_t, idx_scores(_dot_t(qs, ki_new.astype(BF16))), NEG_INF)
    s_all = jnp.concatenate(blocks + [s_new], axis=1)
    hi = jnp.max(s_all, axis=1, keepdims=True)
    lo = jnp.min(jnp.where(s_all == NEG_INF, jnp.inf, s_all), axis=1, keepdims=True)

    def count_ge(mid):
        return jnp.sum(jnp.where(s_all >= mid, 1.0, 0.0), axis=1, keepdims=True)

    thr = _multiway_threshold(count_ge, lo, hi, ksel, 16, BISECT_ITERS // 4)


    q = q_ref[0]
    qst = jnp.concatenate([q[:, h * DSA_DH:(h + 1) * DSA_DH] for h in range(DSA_HEADS)], axis=0).astype(BF16)
    match = (lax.broadcasted_iota(jnp.int32, (rows_q, lanes_kh), 1) % DSA_HEADS
             == lax.broadcasted_iota(jnp.int32, (rows_q, lanes_kh), 0) // tn)
    expand = (lax.broadcasted_iota(jnp.int32, (page, lanes_kh), 1) // DSA_HEADS
              == lax.broadcasted_iota(jnp.int32, (page, lanes_kh), 0)).astype(BF16)
    scale = DSA_DH ** -0.5
    m_acc = jnp.full((rows_q, lanes_kh), NEG_INF, F32)
    for p in range(n_pages):
        sel = jnp.where(blocks[p] >= thr, 1.0, 0.0).astype(BF16)
        sel_kh = jnp.dot(sel, expand, preferred_element_type=F32)
        bias = jnp.where(sel_kh > 0.5, 0.0, NEG_INF)
        bias = jnp.where(match, jnp.concatenate([bias] * DSA_HEADS, axis=0), NEG_INF)
        k2 = k_refs[p][0, 0].reshape(lanes_kh, DSA_DH).astype(BF16)
        r = _dot_t(qst, k2) * scale + bias
        r_ref[:, p * lanes_kh:(p + 1) * lanes_kh] = r
        m_acc = jnp.maximum(m_acc, r)
    pad_rows = jnp.zeros((page - tn, DSA_HEADS * DSA_DH), F32)
    kn = jnp.concatenate([kn_ref[0], pad_rows], axis=0).astype(BF16)
    bias_new = jnp.where(s_new >= thr, 0.0, NEG_INF)
    r_new = jnp.concatenate(
        [_dot_t(qst[h * tn:(h + 1) * tn, :], kn[:, h * DSA_DH:(h + 1) * DSA_DH]) * scale + bias_new
         for h in range(DSA_HEADS)], axis=0)
    m = jnp.maximum(jnp.max(m_acc, axis=1, keepdims=True), jnp.max(r_new, axis=1, keepdims=True))

    l_acc = jnp.zeros((rows_q, lanes_kh), F32)
    acc = jnp.zeros((rows_q, DSA_DH), F32)
    for p in range(n_pages):
        pr = jnp.exp(r_ref[:, p * lanes_kh:(p + 1) * lanes_kh] - m)
        l_acc = l_acc + pr
        v2 = v_refs[p][0, 0].reshape(lanes_kh, DSA_DH).astype(BF16)
        acc = acc + jnp.dot(pr.astype(BF16), v2, preferred_element_type=F32)
    p_new = jnp.exp(r_new - m)
    denom = jnp.sum(l_acc, axis=1, keepdims=True) + jnp.sum(p_new, axis=1, keepdims=True)
    vn = jnp.concatenate([vn_ref[0], pad_rows], axis=0).astype(BF16)
    acc = acc + jnp.concatenate(
        [jnp.dot(p_new[h * tn:(h + 1) * tn, :].astype(BF16), vn[:, h * DSA_DH:(h + 1) * DSA_DH],
                 preferred_element_type=F32) for h in range(DSA_HEADS)], axis=0)
    out = acc / denom
    for h in range(DSA_HEADS):
        o_ref[0, :, h * DSA_DH:(h + 1) * DSA_DH] = out[h * tn:(h + 1) * tn, :]


def dsa_sample(p3, page_ids, layer, cache_k, cache_v, cache_kidx_t):
    db, tn, _ = p3.shape
    n_pages = page_ids.shape[0] // db
    length = n_pages * PAGE_SIZE + tn
    ksel = min(DSA_TOPK_MAX, length // 4)
    w = DSA_HEADS * DSA_DH

    def kv_spec(p):
        return pl.BlockSpec((1, 1, PAGE_SIZE, DSA_HEADS, DSA_DH),
                            lambda i, pt, p=p: (layer, pt[i * n_pages + p], 0, 0, 0))

    def kidx_spec(p):
        return pl.BlockSpec((1, 1, IDX_DIM, PAGE_SIZE), lambda i, pt, p=p: (layer, pt[i * n_pages + p], 0, 0))

    in_specs = [
        pl.BlockSpec((1, tn, w), lambda i, pt: (i, 0, COL_DQ // w)),
        pl.BlockSpec((1, tn, w), lambda i, pt: (i, 0, COL_DQI // w)),
        pl.BlockSpec((1, tn, 128), lambda i, pt: (i, 0, COL_TAIL // 128)),
        pl.BlockSpec((1, tn, w), lambda i, pt: (i, 0, COL_DK // w)),
        pl.BlockSpec((1, tn, w), lambda i, pt: (i, 0, COL_DV // w)),
    ]
    in_specs += [kidx_spec(p) for p in range(n_pages)]
    in_specs += [kv_spec(p) for p in range(n_pages)]
    in_specs += [kv_spec(p) for p in range(n_pages)]
    kern = functools.partial(_dsa_sample_kernel, n_pages=n_pages, ksel=ksel)
    return pl.pallas_call(
        kern,
        grid_spec=pltpu.PrefetchScalarGridSpec(
            num_scalar_prefetch=1,
            grid=(db,),
            in_specs=in_specs,
            out_specs=pl.BlockSpec((1, tn, w), lambda i, pt: (i, 0, 0)),
            scratch_shapes=[pltpu.VMEM((DSA_HEADS * tn, n_pages * PAGE_SIZE * DSA_HEADS), F32)],
        ),
        out_shape=jax.ShapeDtypeStruct((db, tn, w), F32),
        compiler_params=_cparams(("parallel",)),
        name="dsa_sample",
    )(page_ids, p3, p3, p3, p3, p3, *([cache_kidx_t] * n_pages), *([cache_k] * n_pages), *([cache_v] * n_pages))


def _out_proj_kernel(x_ref, gt_ref, ya_ref, yb_ref, w_ref, o_ref):
    wa = ya_ref.shape[-1]
    m = jnp.dot(ya_ref[...].astype(BF16), w_ref[0, :wa, :], preferred_element_type=F32)
    m = m + jnp.dot(yb_ref[...].astype(BF16), w_ref[0, wa:, :], preferred_element_type=F32)
    x = x_ref[...]
    o_ref[...] = x + gt_ref[...] * m.reshape(x.shape)


def out_proj_residual(x3, mod3, mod_row0, layer, ya, yb, w_out16, tg, ts):
    g, s, d = x3.shape
    x_spec, _, mod_specs, mt = _x_specs(g, s, tg, ts, mod_row0, (MOD_GT1,), layer, 0)
    tm = tg * ts
    wa, wb = ya.shape[-1], yb.shape[-1]
    return pl.pallas_call(
        _out_proj_kernel,
        grid=(mt,),
        in_specs=[x_spec, *mod_specs,
                  pl.BlockSpec((tm, wa), lambda i: (i, 0)),
                  pl.BlockSpec((tm, wb), lambda i: (i, 0)),
                  pl.BlockSpec((1, wa + wb, d), lambda i: (layer, 0, 0))],
        out_specs=x_spec,
        out_shape=jax.ShapeDtypeStruct(x3.shape, F32),
        compiler_params=_cparams(("parallel",)),
        name="out_proj",
    )(x3, mod3, ya, yb, w_out16)


def _extract_top(vals_list, n):
    cur = list(vals_list)
    out = []
    for _ in range(n):
        m = cur[0].max(axis=0, keepdims=True)
        for c in cur[1:]:
            m = jnp.maximum(m, c.max(axis=0, keepdims=True))
        out.append(m)
        cur = [jnp.where(c >= m, NEG_INF, c) for c in cur]
    return out


PEER_STAIR = tuple(PEER_TOPK // (a + 1) for a in range(PEER_TOPK))


def _extract_top_ranked(s, n):
    cur = s
    rank = jnp.full(s.shape, float(n), F32)
    out = []
    for r in range(n):
        m = cur.max(axis=0, keepdims=True)
        out.append(m)
        hit = cur >= m
        rank = jnp.where(hit, float(r), rank)
        cur = jnp.where(hit, NEG_INF, cur)
    return out, rank


def _peer_route_kernel(qt_ref, keys_ref, m1_ref, e1_ref, r2_ref, e2_ref):
    half = PEER_DQ // 2
    raw = [jnp.dot(keys_ref[0, p], qt_ref[p * half:(p + 1) * half, :], precision=HIGHEST,
                   preferred_element_type=F32) for p in range(2)]
    v1, rank1 = _extract_top_ranked(raw[0], PEER_TOPK)
    v2, rank2 = _extract_top_ranked(raw[1], PEER_TOPK)
    v2_all = jnp.concatenate(v2, axis=0)
    cand = [v1[a] + v2_all[:PEER_STAIR[a], :] for a in range(PEER_TOPK)]
    ctop = _extract_top(cand, PEER_TOPK)
    cmax, cut_c = ctop[0], ctop[PEER_TOPK - 1]
    z = jnp.zeros_like(cmax)
    for c in ctop:
        z = z + jnp.exp(c - cmax)
    m1 = jnp.zeros(raw[0].shape, F32)
    for a in range(PEER_TOPK):
        n_sel = jnp.sum(jnp.where(cand[a] >= cut_c, 1.0, 0.0), axis=0, keepdims=True)
        m1 = jnp.where(rank1 == float(a), n_sel, m1)
    m1_ref[...] = m1
    e1_ref[...] = jnp.where(rank1 < float(PEER_TOPK), jnp.exp(raw[0] - v1[0]), 0.0)
    r2_ref[...] = rank2.astype(r2_ref.dtype)
    e2_ref[...] = (jnp.where(rank2 < float(PEER_TOPK), jnp.exp(raw[1] - v2[0]), 0.0) / z).astype(e2_ref.dtype)


def peer_route(qt, keys, layer, tm=512):
    n = qt.shape[1]
    rows = PEER_HEADS * PEER_NKEYS
    blk = pl.BlockSpec((PEER_NKEYS, tm), lambda i, h: (h, i))
    shp = jax.ShapeDtypeStruct((rows, n), F32)
    shp16 = jax.ShapeDtypeStruct((rows, n), BF16)
    return pl.pallas_call(
        _peer_route_kernel,
        grid=(n // tm, PEER_HEADS),
        in_specs=[pl.BlockSpec((PEER_DQ, tm), lambda i, h: (h, i)),
                  pl.BlockSpec((1, 2, PEER_NKEYS, PEER_DQ // 2), lambda i, h: (layer, 0, 0, 0))],
        out_specs=[blk, blk, blk, blk],
        out_shape=[shp, shp, shp16, shp16],
        compiler_params=_cparams(("parallel", "parallel")),
        name="peer_route",
    )(qt, keys)


PEER_SUB = 256


def _peer_dense_kernel(x_ref, gt_ref, h_ref, m1_ref, e1_ref, r2_ref, e2_ref, u_ref, v_ref, o_ref, acc_ref, *, eb):
    j = pl.program_id(1)

    @pl.when(j == 0)
    def _():
        acc_ref[...] = jnp.zeros_like(acc_ref)

    assert eb // PEER_NKEYS == 8 and PEER_SUB % PEER_NKEYS == 0
    h_tok = h_ref[...]
    gdt = r2_ref.dtype
    m1_rows, e1_rows = [], []
    for h in range(PEER_HEADS):
        rows = pl.ds(pl.multiple_of(h * PEER_NKEYS + j * 8, 8), 8)
        m1_rows.append(m1_ref[rows, :].astype(gdt))
        e1_rows.append(e1_ref[rows, :].astype(gdt))
    total = None
    for c in range(eb // PEER_SUB):
        esl = slice(c * PEER_SUB, (c + 1) * PEER_SUB)
        act = jax.nn.gelu(_dot_t(u_ref[0, esl, :], h_tok))
        coef_rows = []
        for rr in range(PEER_SUB // PEER_NKEYS):
            r = c * (PEER_SUB // PEER_NKEYS) + rr
            g = jnp.zeros((PEER_NKEYS, act.shape[1]), gdt)
            for h in range(PEER_HEADS):
                hs = slice(h * PEER_NKEYS, (h + 1) * PEER_NKEYS)
                picked = jnp.where(r2_ref[hs, :] < m1_rows[h][r:r + 1, :], e2_ref[hs, :], jnp.zeros((), gdt))
                g = g + picked * e1_rows[h][r:r + 1, :]
            coef_rows.append(g.astype(F32) * act[rr * PEER_NKEYS:(rr + 1) * PEER_NKEYS, :])
        coef_t = jnp.concatenate(coef_rows, axis=0)
        part = jnp.dot(jnp.transpose(coef_t).astype(BF16), v_ref[0, esl, :], preferred_element_type=F32)
        total = part if total is None else total + part
    acc_ref[...] += total

    @pl.when(j == pl.num_programs(1) - 1)
    def _():
        x = x_ref[...]
        o_ref[...] = x + gt_ref[...] * acc_ref[...].reshape(x.shape)


def peer_dense(x3, mod3, mod_row0, layer, h16, m1, e1, r2, e2, u16, v16, tg, ts, eb=1024):
    g, s, d = x3.shape
    x_spec, _, mod_specs, mt = _x_specs(g, s, tg, ts, mod_row0, (MOD_GT2,), layer, 0)
    tm = tg * ts
    rows = PEER_HEADS * PEER_NKEYS
    once = pl.Buffered(1)
    tok = pl.BlockSpec((rows, tm), lambda i, j: (0, i), pipeline_mode=once)
    kern = functools.partial(_peer_dense_kernel, eb=eb)
    return pl.pallas_call(
        kern,
        grid=(mt, PEER_EXPERTS // eb),
        in_specs=[x_spec, *mod_specs,
                  pl.BlockSpec((tm, d), lambda i, j: (i, 0), pipeline_mode=once),
                  tok, tok, tok, tok,
                  pl.BlockSpec((1, eb, d), lambda i, j: (layer, j, 0)),
                  pl.BlockSpec((1, eb, d), lambda i, j: (layer, j, 0))],
        out_specs=x_spec,
        out_shape=jax.ShapeDtypeStruct(x3.shape, F32),
        scratch_shapes=[pltpu.VMEM((tm, d), F32)],
        compiler_params=_cparams(("parallel", "arbitrary")),
        name="peer_dense",
    )(x3, mod3, h16, m1, e1, r2, e2, u16, v16)


def _final_norm_kernel(x_ref, g_ref, o_ref):
    x = x_ref[...]
    o_ref[...] = x * lax.rsqrt(jnp.mean(x * x, axis=-1, keepdims=True) + RMS_EPS) * g_ref[...]


def final_norm(x3, g_final, tg, ts):
    g, s, d = x3.shape
    ns = s // ts
    spec = pl.BlockSpec((tg, ts, d), lambda i: (i // ns, i % ns, 0))
    return pl.pallas_call(
        _final_norm_kernel,
        grid=((g // tg) * ns,),
        in_specs=[spec, pl.BlockSpec((1, 1, d), lambda i: (0, 0, 0))],
        out_specs=spec,
        out_shape=jax.ShapeDtypeStruct(x3.shape, F32),
        compiler_params=_cparams(("parallel",)),
        name="final_norm",
    )(x3, g_final.reshape(1, 1, d))


def _prep_w_in(w_in):
    o = [0]
    for sz in SPLIT_SIZES:
        o.append(o[-1] + sz)
    pad = jnp.zeros(w_in.shape[:-1] + (NP_IN - o[-1],), w_in.dtype)
    parts = [w_in[..., o[0]:o[4]], w_in[..., o[5]:o[9]], w_in[..., o[9]:o[10]], w_in[..., o[4]:o[5]],
             w_in[..., o[10]:o[11]], pad]
    return jnp.concatenate(parts, axis=-1).astype(BF16)


def _prep_w_a2(w_a2):
    z = jnp.zeros((DEPTH, 128, w_a2.shape[-1]), F32)
    return z.at[:, TAIL_LR:TAIL_LR + GLA_LR, :].set(w_a2)


def _stream(x3, mod3, mod_row0, tg, ts, weights, dsa_fn, dsa_wants_bf16, s0_fn, gla_tt):
    g, s, d = x3.shape
    news = []
    for l in range(DEPTH):
        proj = norm_proj(x3, weights["g_mix"], mod3, l * MOD_ROWS + mod_row0, l, weights["w_in"], tg, ts, NP_IN // 3,
                         dsa_wants_bf16)
        p3 = proj[0].reshape(g, s, NP_IN)
        s0, s0_row0 = s0_fn(l)
        ya, s_new = gla_mixer(p3, weights["w_a2"][l], weights["b_a"][l:l + 1], weights["g_gla"][l:l + 1],
                              s0, s0_row0, gla_tt)
        yb = dsa_fn(p3, proj[1].reshape(g, s, NP_IN) if dsa_wants_bf16 else None, l)
        k_b = p3[..., COL_DK:COL_DK + 1024].reshape(g, s, DSA_HEADS, DSA_DH)
        v_b = p3[..., COL_DV:COL_DV + 1024].reshape(g, s, DSA_HEADS, DSA_DH)
        ki = p3[..., COL_TAIL + TAIL_KI:COL_TAIL + TAIL_KI + IDX_DIM]
        news.append((k_b, v_b, ki, s_new))
        x3 = out_proj_residual(x3, mod3, l * MOD_ROWS + mod_row0, l, ya.reshape(g * s, -1), yb.reshape(g * s, -1),
                               weights["w_out"], tg, ts)
        qt, h16 = norm_proj_t(x3, weights["g_ffn"], mod3, l * MOD_ROWS + mod_row0, l, weights["peer_wq_t"], tg, ts)
        m1, e1, r2, e2 = peer_route(qt, weights["peer_keys"], l)
        x3 = peer_dense(x3, mod3, l * MOD_ROWS + mod_row0, l, h16, m1, e1, r2, e2,
                        weights["peer_u"], weights["peer_v"], tg, ts)
    y = final_norm(x3, weights["g_final"], tg, ts)
    return y, [jnp.stack([n[i] for n in news]) for i in range(4)]


def kernel(x_prompt, x_sample, c_prompt, c_sample, cache_k, cache_v, cache_kidx, state_gla, page_table, w_ada, b_ada,
           g_mix, g_ffn, w_in, w_a2, b_a, g_gla, w_out, peer_wq, peer_keys, peer_u, peer_v, g_final):
    batch, seq, d = x_prompt.shape
    db, tn, _ = x_sample.shape
    c_all = jnp.concatenate([c_sample, c_prompt, jnp.zeros((MOD_ROWS - db - batch, d), F32)], axis=0)
    mod3 = ada_mod(c_all, w_ada, b_ada).reshape(DEPTH * MOD_ROWS, 1, 6 * d)

    weights = dict(
        g_mix=g_mix.reshape(DEPTH, 1, d), g_ffn=g_ffn.reshape(DEPTH, 1, d),
        w_in=_prep_w_in(w_in), w_a2=_prep_w_a2(w_a2), b_a=b_a, g_gla=g_gla,
        w_out=w_out.astype(BF16), peer_wq_t=jnp.swapaxes(peer_wq, 1, 2).astype(BF16),
        peer_keys=peer_keys, peer_u=peer_u.astype(BF16), peer_v=peer_v.astype(BF16), g_final=g_final,
    )

    def dsa_p(p3, pb3, l):
        return dsa_prompt(pb3, p3)

    page_ids = page_table.astype(jnp.int32).reshape(-1)
    kidx_t = jnp.swapaxes(cache_kidx, 2, 3)

    def dsa_s(p3, pb3, l):
        return dsa_sample(p3, page_ids, l, cache_k, cache_v, kidx_t)

    zero_state = jnp.zeros((batch, GLA_HEADS, GLA_DK, GLA_DV), F32)
    state2 = state_gla.reshape(DEPTH * db, GLA_HEADS, GLA_DK, GLA_DV)
    y_p, sp = _stream(x_prompt, mod3, MOD_PROMPT_ROW0, 1, 512, weights, dsa_p, True, lambda l: (zero_state, 0), 512)
    y_s, ss = _stream(x_sample, mod3, 0, 64, tn, weights, dsa_s, False, lambda l: (state2, l * db), tn)
    return (y_p, y_s, sp[0], sp[1], sp[2], sp[3], ss[0], ss[1], ss[2], ss[3])
```
